```python
import jax, jax.numpy as jnp
from jax import lax
import numpy as np

D_MODEL = 2048
BATCH = 4
SEQ = 8192
DEPTH = 1

GRID_W = 64
EPS = 1e-6
ROPE_THETA = 10000.0
DA_HEADS = 4
DA_HEAD_DIM = 128
DA_WIDTH = DA_HEADS * 2 * DA_HEAD_DIM
Q_BLOCK = 128
NA_HEADS = 8
NA_HEAD_DIM = 128
NA_WIDTH = NA_HEADS * NA_HEAD_DIM
NA_KH_MAX = 8
NA_KW = 16
D_FF = 4 * D_MODEL
N_BRANCHES = 2
IN_SPLITS = [DA_WIDTH, DA_WIDTH, DA_WIDTH, NA_WIDTH, NA_WIDTH, NA_WIDTH, N_BRANCHES * D_MODEL]
IN_COLS = sum(IN_SPLITS)

kernel_name = "hybrid_diffattn_neighattn_gated_encoder"


def rms_norm(x, w):
    xf = x.astype(jnp.float32)
    y = xf * lax.rsqrt(jnp.mean(xf * xf, axis=-1, keepdims=True) + EPS)
    return (y * w.astype(jnp.float32)).astype(x.dtype)


def rope_tables(seq, dim):
    inv = 1.0 / (ROPE_THETA ** (jnp.arange(0, dim, 2, dtype=jnp.float32) / dim))
    ang = jnp.arange(seq, dtype=jnp.float32)[:, None] * inv[None, :]
    return jnp.cos(ang), jnp.sin(ang)


def apply_rope(x, cos, sin):
    xf = x.astype(jnp.float32)
    x1, x2 = jnp.split(xf, 2, axis=-1)
    c = cos[:, None, None, :]
    s = sin[:, None, None, :]
    out = jnp.concatenate([x1 * c - x2 * s, x2 * c + x1 * s], axis=-1)
    return out.astype(x.dtype)


def diff_attention(q, k, v, lam):
    B, S, H, _, d = q.shape
    nb = S // Q_BLOCK
    qb = q.reshape(B, nb, Q_BLOCK, H, 2, d).transpose(1, 0, 2, 3, 4, 5)
    scale = d ** -0.5

    def block(qi):
        s = jnp.einsum('bqhcd,bkhcd->bhcqk', qi, k, preferred_element_type=jnp.float32) * scale
        p = jax.nn.softmax(s, axis=-1)
        p = p[:, :, 0] - lam * p[:, :, 1]
        return jnp.einsum('bhqk,bkhe->bqhe', p.astype(v.dtype), v)

    o = lax.map(block, qb)
    return o.transpose(1, 0, 2, 3, 4).reshape(B, S, H, 2 * d)


def neighbourhood_attention(q, k, v, rpb):
    B, S, H, d = q.shape
    rows = S // GRID_W
    kh = min(NA_KH_MAX, rows)
    qg = q.reshape(B, rows, GRID_W, H, d).transpose(1, 0, 2, 3, 4)
    kg = k.reshape(B, rows, GRID_W, H, d)
    vg = v.reshape(B, rows, GRID_W, H, d)
    col = jnp.arange(GRID_W)
    col_start = jnp.clip(col - NA_KW // 2, 0, GRID_W - NA_KW)
    col_idx = col_start[:, None] + jnp.arange(NA_KW)[None, :]
    dc = col_idx - col[:, None] + (NA_KW - 1)
    row_ids = jnp.arange(rows)
    row_start = jnp.clip(row_ids - kh // 2, 0, rows - kh)
    scale = d ** -0.5

    def row_block(args):
        qr, r, rs = args
        kb = lax.dynamic_slice_in_dim(kg, rs, kh, axis=1)
        vb = lax.dynamic_slice_in_dim(vg, rs, kh, axis=1)
        kw = kb[:, :, col_idx]
        vw = vb[:, :, col_idx]
        dr = rs + jnp.arange(kh) - r + (NA_KH_MAX - 1)
        bias = rpb[:, dr[None, :, None], dc[:, None, :]]
        s = jnp.einsum('bqhd,bnqwhd->bhqnw', qr, kw, preferred_element_type=jnp.float32) * scale
        s = s + bias.astype(jnp.float32)[None]
        p = jax.nn.softmax(s, axis=(-2, -1))
        return jnp.einsum('bhqnw,bnqwhd->bqhd', p.astype(v.dtype), vw)

    o = lax.map(row_block, (qg, row_ids, row_start))
    return o.transpose(1, 0, 2, 3, 4).reshape(B, S, H, d)


def setup_inputs(seed: int = 0) -> dict:
    key = jax.random.key(seed)
    ks = jax.random.split(key, 20)
    f32 = jnp.float32

    def nrm(k, shape, scale):
        return jax.random.normal(k, shape, f32) * scale

    def gain(k, n):
        return 1.0 + 0.05 * jax.random.normal(k, (DEPTH, n), f32)

    return {
        "x": jax.random.normal(ks[0], (BATCH, SEQ, D_MODEL), f32),
        "w_in": nrm(ks[1], (DEPTH, D_MODEL, IN_COLS), D_MODEL ** -0.5),
        "w_branch_a": nrm(ks[2], (DEPTH, DA_WIDTH, D_MODEL), DA_WIDTH ** -0.5),
        "w_branch_b": nrm(ks[3], (DEPTH, NA_WIDTH, D_MODEL), NA_WIDTH ** -0.5),
        "w_out": nrm(ks[4], (DEPTH, D_MODEL, D_MODEL), D_MODEL ** -0.5),
        "norm_mix_pre": gain(ks[5], D_MODEL),
        "norm_mix_post": gain(ks[6], D_MODEL),
        "norm_mlp_pre": gain(ks[7], D_MODEL),
        "norm_mlp_post": gain(ks[8], D_MODEL),
        "lam_q1": nrm(ks[9], (DEPTH, DA_HEAD_DIM), 0.1),
        "lam_k1": nrm(ks[10], (DEPTH, DA_HEAD_DIM), 0.1),
        "lam_q2": nrm(ks[11], (DEPTH, DA_HEAD_DIM), 0.1),
        "lam_k2": nrm(ks[12], (DEPTH, DA_HEAD_DIM), 0.1),
        "subln_w": gain(ks[13], 2 * DA_HEAD_DIM),
        "na_rpb": nrm(ks[14], (DEPTH, NA_HEADS, 2 * NA_KH_MAX - 1, 2 * NA_KW - 1), 0.1),
        "w_up": nrm(ks[15], (DEPTH, D_MODEL, D_FF), D_MODEL ** -0.5),
        "w_down": nrm(ks[16], (DEPTH, D_FF, D_MODEL), D_FF ** -0.5),
    }


def reference(x, w_in, w_branch_a, w_branch_b, w_out, norm_mix_pre, norm_mix_post,
              norm_mlp_pre, norm_mlp_post, lam_q1, lam_k1, lam_q2, lam_k2, subln_w,
              na_rpb, w_up, w_down):
    B, S, _ = x.shape
    cos, sin = rope_tables(S, DA_HEAD_DIM)
    split_at = [int(c) for c in np.cumsum(IN_SPLITS)[:-1]]
    for l in range(DEPTH):
        lambda_init = 0.8 - 0.6 * float(np.exp(-0.3 * l))
        h = rms_norm(x, norm_mix_pre[l])
        proj = jnp.einsum('bsd,dc->bsc', h, w_in[l])
        qa, ka, va, qn, kn, vn, gates = jnp.split(proj, split_at, axis=-1)
        qa = apply_rope(qa.reshape(B, S, DA_HEADS, 2, DA_HEAD_DIM), cos, sin)
        ka = apply_rope(ka.reshape(B, S, DA_HEADS, 2, DA_HEAD_DIM), cos, sin)
        va = va.reshape(B, S, DA_HEADS, 2 * DA_HEAD_DIM)
        lam = (jnp.exp(jnp.sum(lam_q1[l].astype(jnp.float32) * lam_k1[l].astype(jnp.float32)))
               - jnp.exp(jnp.sum(lam_q2[l].astype(jnp.float32) * lam_k2[l].astype(jnp.float32)))
               + lambda_init)
        oa = diff_attention(qa, ka, va, lam)
        oa = (rms_norm(oa, subln_w[l]) * (1.0 - lambda_init)).reshape(B, S, DA_WIDTH)
        on = neighbourhood_attention(qn.reshape(B, S, NA_HEADS, NA_HEAD_DIM),
                                     kn.reshape(B, S, NA_HEADS, NA_HEAD_DIM),
                                     vn.reshape(B, S, NA_HEADS, NA_HEAD_DIM),
                                     na_rpb[l]).reshape(B, S, NA_WIDTH)
        g_a, g_b = jnp.split(jax.nn.sigmoid(gates), N_BRANCHES, axis=-1)
        mixed = (g_a * jnp.einsum('bsc,cd->bsd', oa, w_branch_a[l])
                 + g_b * jnp.einsum('bsc,cd->bsd', on, w_branch_b[l]))
        y = jnp.einsum('bsd,de->bse', mixed, w_out[l])
        x = x + rms_norm(y, norm_mix_post[l])
        h = rms_norm(x, norm_mlp_pre[l])
        u = jnp.square(jax.nn.relu(jnp.einsum('bsd,df->bsf', h, w_up[l])))
        x = x + rms_norm(jnp.einsum('bsf,fd->bsd', u, w_down[l]), norm_mlp_post[l])
    return x
```

```python
import functools
import math

import jax
import jax.numpy as jnp
from jax import lax
from jax.experimental import pallas as pl
from jax.experimental.pallas import tpu as pltpu

EPS = 1e-6
ROPE_THETA = 10000.0
GRID_W = 64
DA_HEADS = 4
DA_HEAD_DIM = 128
NA_HEADS = 8
NA_HEAD_DIM = 128
NA_KH = 8
NA_KW = 16
LANES = 128
NEG_BIG = -1e30
NA_QROWS = 4
NA_KROWS = 12
V7X_VMEM_BYTES = 64 * 1024 * 1024

_NT = (((1,), (1,)), ((), ()))


def _vmem_limit(block_bytes):
    return int(min(block_bytes + 16 * 1024 * 1024, V7X_VMEM_BYTES - 6 * 1024 * 1024))


def _rms(x, g):
    return x * lax.rsqrt(jnp.mean(x * x, axis=-1, keepdims=True) + EPS) * g


def _qkv_kernel(x_ref, g_ref, w_ref, cos_ref, sin_ref, o_ref, h_ref, *, n_rope_tiles):
    j = pl.program_id(1)

    @pl.when(j == 0)
    def _():
        h_ref[...] = _rms(x_ref[...], g_ref[...]).astype(h_ref.dtype)

    acc = jnp.dot(h_ref[...], w_ref[...], preferred_element_type=jnp.float32)

    @pl.when(j < n_rope_tiles)
    def _():
        c = cos_ref[...]
        s = sin_ref[...]
        for g in range(acc.shape[1] // LANES):
            y = acc[:, g * LANES:(g + 1) * LANES]
            o_ref[:, g * LANES:(g + 1) * LANES] = (
                y * c + pltpu.roll(y, LANES // 2, 1) * s).astype(o_ref.dtype)

    @pl.when(j >= n_rope_tiles)
    def _():
        o_ref[...] = acc.astype(o_ref.dtype)


def _qkv_proj(x2, gain, w_bf16, cosd, sind, seq, *, tm, tn, n_cols, n_rope_cols):
    m, d = x2.shape
    pos_blocks = seq // tm
    blk = 2 * (tm * d * 4 + d * tn * 2 + tm * tn * 2 + 2 * tm * LANES * 4) + tm * d * 2 + tm * tn * 4
    return pl.pallas_call(
        functools.partial(_qkv_kernel, n_rope_tiles=n_rope_cols // tn),
        out_shape=jax.ShapeDtypeStruct((m, n_cols), jnp.bfloat16),
        grid=(m // tm, n_cols // tn),
        in_specs=[
            pl.BlockSpec((tm, d), lambda i, j: (i, 0)),
            pl.BlockSpec((1, d), lambda i, j: (0, 0)),
            pl.BlockSpec((d, tn), lambda i, j: (0, j)),
            pl.BlockSpec((tm, LANES), lambda i, j: (i % pos_blocks, 0)),
            pl.BlockSpec((tm, LANES), lambda i, j: (i % pos_blocks, 0)),
        ],
        out_specs=pl.BlockSpec((tm, tn), lambda i, j: (i, j)),
        scratch_shapes=[pltpu.VMEM((tm, d), jnp.bfloat16)],
        compiler_params=pltpu.CompilerParams(
            dimension_semantics=("parallel", "arbitrary"),
            vmem_limit_bytes=_vmem_limit(blk)),
        name="qkv_proj",
    )(x2, gain, w_bf16, cosd, sind)


def _diff_attn_kernel(lq1_ref, lk1_ref, lq2_ref, lk2_ref, subw_ref, q_ref, k_ref, v_ref, o_ref,
                      m1, l1, a1, m2, l2, a2, *, tk, scale, lambda_init):
    seq = k_ref.shape[1]
    d = DA_HEAD_DIM
    q = q_ref[0]
    q_halves = (q[:, :d], q[:, d:])
    state = ((m1, l1, a1), (m2, l2, a2))
    for m_ref, l_ref, a_ref in state:
        m_ref[...] = jnp.full(m_ref.shape, NEG_BIG, jnp.float32)
        l_ref[...] = jnp.zeros(l_ref.shape, jnp.float32)
        a_ref[...] = jnp.zeros(a_ref.shape, jnp.float32)

    def body(c, carry):
        off = pl.multiple_of(c * tk, tk)
        k = k_ref[0, pl.ds(off, tk), :]
        v = v_ref[0, pl.ds(off, tk), :]
        for half, (m_ref, l_ref, a_ref) in enumerate(state):
            s = lax.dot_general(q_halves[half], k[:, half * d:(half + 1) * d], _NT,
                                preferred_element_type=jnp.float32) * scale
            m_prev = m_ref[...]
            m_new = jnp.maximum(m_prev, jnp.max(s, axis=1, keepdims=True))
            alpha = jnp.exp(m_prev - m_new)
            p = jnp.exp(s - jnp.concatenate([m_new] * (tk // LANES), axis=1))
            psum = p[:, :LANES]
            for t in range(1, tk // LANES):
                psum = psum + p[:, t * LANES:(t + 1) * LANES]
            l_ref[...] = alpha * l_ref[...] + psum
            pv = jnp.dot(p.astype(v.dtype), v, preferred_element_type=jnp.float32)
            a_ref[...] = a_ref[...] * jnp.concatenate([alpha] * (2 * d // LANES), axis=1) + pv
            m_ref[...] = m_new
        return carry

    lax.fori_loop(0, seq // tk, body, 0)

    lam = (jnp.exp(jnp.sum(lq1_ref[...] * lk1_ref[...])) - jnp.exp(jnp.sum(lq2_ref[...] * lk2_ref[...]))
           + lambda_init)
    o1 = a1[...] / jnp.sum(l1[...], axis=1, keepdims=True)
    o2 = a2[...] / jnp.sum(l2[...], axis=1, keepdims=True)
    o = o1 - lam * o2
    o = _rms(o, subw_ref[...]) * (1.0 - lambda_init)
    o_ref[0] = o.astype(o_ref.dtype)


def _diff_attn(qkv3, lq1, lk1, lq2, lk2, subw, *, tq, tk, lambda_init):
    b, seq, _ = qkv3.shape
    hd = 2 * DA_HEAD_DIM
    kern = functools.partial(_diff_attn_kernel, tk=tk, scale=DA_HEAD_DIM ** -0.5, lambda_init=lambda_init)
    vec = lambda n: pl.BlockSpec((1, n), lambda bi, h, qi: (0, 0))
    blk = 2 * (2 * seq * hd * 2 + 2 * tq * hd * 2) + 2 * (2 * tq * LANES * 4 + tq * hd * 4) + 4 * tq * tk * 4
    return pl.pallas_call(
        kern,
        out_shape=jax.ShapeDtypeStruct((b, seq, DA_HEADS * hd), jnp.bfloat16),
        grid=(b, DA_HEADS, seq // tq),
        in_specs=[
            vec(DA_HEAD_DIM), vec(DA_HEAD_DIM), vec(DA_HEAD_DIM), vec(DA_HEAD_DIM), vec(hd),
            pl.BlockSpec((1, tq, hd), lambda bi, h, qi: (bi, qi, h)),
            pl.BlockSpec((1, seq, hd), lambda bi, h, qi: (bi, 0, DA_HEADS + h)),
            pl.BlockSpec((1, seq, hd), lambda bi, h, qi: (bi, 0, 2 * DA_HEADS + h)),
        ],
        out_specs=pl.BlockSpec((1, tq, hd), lambda bi, h, qi: (bi, qi, h)),
        scratch_shapes=[
            pltpu.VMEM((tq, LANES), jnp.float32), pltpu.VMEM((tq, LANES), jnp.float32),
            pltpu.VMEM((tq, hd), jnp.float32),
            pltpu.VMEM((tq, LANES), jnp.float32), pltpu.VMEM((tq, LANES), jnp.float32),
            pltpu.VMEM((tq, hd), jnp.float32),
        ],
        compiler_params=pltpu.CompilerParams(
            dimension_semantics=("parallel", "parallel", "arbitrary"),
            vmem_limit_bytes=_vmem_limit(blk)),
        name="diff_attn",
    )(lq1, lk1, lq2, lk2, subw, qkv3, qkv3, qkv3)


def _neigh_bias(rpb, rows):
    nq = NA_QROWS * GRID_W
    nk = NA_KROWS * GRID_W
    qi = jnp.arange(nq)
    kj = jnp.arange(nk)
    qr, qc = qi // GRID_W, qi % GRID_W
    kr, kc = kj // GRID_W, kj % GRID_W
    col_start = jnp.clip(qc - NA_KW // 2, 0, GRID_W - NA_KW)
    col_ok = (kc[None, :] >= col_start[:, None]) & (kc[None, :] < col_start[:, None] + NA_KW)
    dc = kc[None, :] - qc[:, None] + (NA_KW - 1)
    nblk = rows // NA_QROWS
    tables = []
    for rb, key_row0 in ((0, 0), (1, 0), (nblk - 1, (nblk - 3) * NA_QROWS)):
        r = rb * NA_QROWS + qr
        krow = key_row0 + kr
        row_start = jnp.clip(r - NA_KH // 2, 0, rows - NA_KH)
        row_ok = (krow[None, :] >= row_start[:, None]) & (krow[None, :] < row_start[:, None] + NA_KH)
        dr = krow[None, :] - r[:, None] + (NA_KH - 1)
        ok = row_ok & col_ok
        bias = rpb[:, jnp.clip(dr, 0, 2 * NA_KH - 2), jnp.clip(dc, 0, 2 * NA_KW - 2)]
        tables.append(jnp.where(ok[None], bias.astype(jnp.float32), NEG_BIG))
    return jnp.stack(tables)


def _neigh_attn_kernel(bias_ref, q_ref, k0_ref, k1_ref, k2_ref, v0_ref, v1_ref, v2_ref, o_ref, *, scale):
    rb = pl.program_id(2)
    nblk = pl.num_programs(2)
    cls = jnp.where(rb == 0, 0, jnp.where(rb == nblk - 1, 2, 1))
    q = q_ref[0]
    s = jnp.concatenate(
        [lax.dot_general(q, k_ref[0], _NT, preferred_element_type=jnp.float32)
         for k_ref in (k0_ref, k1_ref, k2_ref)], axis=1)
    s = s * scale + bias_ref[cls, 0]
    p = jnp.exp(s - jnp.max(s, axis=1, keepdims=True))
    l = jnp.sum(p, axis=1, keepdims=True)
    pb = p.astype(v0_ref.dtype)
    nkb = q.shape[0]
    o = None
    for t, v_ref in enumerate((v0_ref, v1_ref, v2_ref)):
        pv = jnp.dot(pb[:, t * nkb:(t + 1) * nkb], v_ref[0], preferred_element_type=jnp.float32)
        o = pv if o is None else o + pv
    o_ref[0] = (o / l).astype(o_ref.dtype)


def _neigh_attn(qkv3, bias):
    b, seq, _ = qkv3.shape
    d = NA_HEAD_DIM
    nq = NA_QROWS * GRID_W
    nblk = seq // nq
    col0 = 3 * DA_HEADS * 2 * DA_HEAD_DIM // d
    start = lambda rb: jnp.clip(rb - 1, 0, nblk - 3)
    kv_spec = lambda base, t: pl.BlockSpec(
        (1, nq, d), lambda h, bi, rb: (bi, start(rb) + t, col0 + base * NA_HEADS + h))
    blk = 2 * (3 * nq * NA_KROWS * GRID_W * 4 + 8 * nq * d * 2) + 6 * nq * NA_KROWS * GRID_W * 4
    return pl.pallas_call(
        functools.partial(_neigh_attn_kernel, scale=d ** -0.5),
        out_shape=jax.ShapeDtypeStruct((b, seq, NA_HEADS * d), jnp.bfloat16),
        grid=(NA_HEADS, b, nblk),
        in_specs=[
            pl.BlockSpec((3, 1, nq, NA_KROWS * GRID_W), lambda h, bi, rb: (0, h, 0, 0)),
            pl.BlockSpec((1, nq, d), lambda h, bi, rb: (bi, rb, col0 + h)),
            kv_spec(1, 0), kv_spec(1, 1), kv_spec(1, 2),
            kv_spec(2, 0), kv_spec(2, 1), kv_spec(2, 2),
        ],
        out_specs=pl.BlockSpec((1, nq, d), lambda h, bi, rb: (bi, rb, h)),
        compiler_params=pltpu.CompilerParams(
            dimension_semantics=("parallel", "parallel", "arbitrary"),
            vmem_limit_bytes=_vmem_limit(blk)),
        name="neigh_attn",
    )(bias, qkv3, qkv3, qkv3, qkv3, qkv3, qkv3, qkv3)


def _merge_kernel(x_ref, gpre_ref, gpost_ref, oa_ref, on_ref, wga_ref, wgb_ref, wa_ref, wb_ref, wo_ref,
                  o_ref, h_ref, y_ref):
    j = pl.program_id(1)

    @pl.when(j == 0)
    def _():
        h_ref[...] = _rms(x_ref[...], gpre_ref[...]).astype(h_ref.dtype)
        y_ref[...] = jnp.zeros(y_ref.shape, jnp.float32)

    h = h_ref[...]
    f32 = jnp.float32
    ga = 1.0 / (1.0 + jnp.exp(-jnp.dot(h, wga_ref[...], preferred_element_type=f32)))
    gb = 1.0 / (1.0 + jnp.exp(-jnp.dot(h, wgb_ref[...], preferred_element_type=f32)))
    a = jnp.dot(oa_ref[...], wa_ref[...], preferred_element_type=f32)
    b = jnp.dot(on_ref[...], wb_ref[...], preferred_element_type=f32)
    mixed = (ga * a + gb * b).astype(wo_ref.dtype)
    y_ref[...] += jnp.dot(mixed, wo_ref[...], preferred_element_type=f32)

    @pl.when(j == pl.num_programs(1) - 1)
    def _():
        o_ref[...] = x_ref[...] + _rms(y_ref[...], gpost_ref[...])


def _merge(x2, gpre, gpost, oa2, on2, w_in_bf16, wa, wb, wo, *, tm, tn):
    m, d = x2.shape
    ca = oa2.shape[1]
    cb = on2.shape[1]
    gate_col0 = (w_in_bf16.shape[1] - 2 * d) // tn
    nj = d // tn
    blk = (2 * (2 * tm * d * 4 + tm * (ca + cb) * 2 + (2 * d + ca + cb + d) * tn * 2)
           + tm * d * 2 + tm * d * 4 + 6 * tm * tn * 4)
    return pl.pallas_call(
        _merge_kernel,
        out_shape=jax.ShapeDtypeStruct((m, d), jnp.float32),
        grid=(m // tm, nj),
        in_specs=[
            pl.BlockSpec((tm, d), lambda i, j: (i, 0)),
            pl.BlockSpec((1, d), lambda i, j: (0, 0)),
            pl.BlockSpec((1, d), lambda i, j: (0, 0)),
            pl.BlockSpec((tm, ca), lambda i, j: (i, 0)),
            pl.BlockSpec((tm, cb), lambda i, j: (i, 0)),
            pl.BlockSpec((d, tn), lambda i, j: (0, gate_col0 + j)),
            pl.BlockSpec((d, tn), lambda i, j: (0, gate_col0 + nj + j)),
            pl.BlockSpec((ca, tn), lambda i, j: (0, j)),
            pl.BlockSpec((cb, tn), lambda i, j: (0, j)),
            pl.BlockSpec((tn, d), lambda i, j: (j, 0)),
        ],
        out_specs=pl.BlockSpec((tm, d), lambda i, j: (i, 0)),
        scratch_shapes=[pltpu.VMEM((tm, d), jnp.bfloat16), pltpu.VMEM((tm, d), jnp.float32)],
        compiler_params=pltpu.CompilerParams(
            dimension_semantics=("parallel", "arbitrary"),
            vmem_limit_bytes=_vmem_limit(blk)),
        name="merge",
    )(x2, gpre, gpost, oa2, on2, w_in_bf16, w_in_bf16, wa, wb, wo)


def _mlp_kernel(x_ref, gpre_ref, gpost_ref, wu_ref, wd_ref, o_ref, h_ref, acc_ref):
    j = pl.program_id(1)

    @pl.when(j == 0)
    def _():
        h_ref[...] = _rms(x_ref[...], gpre_ref[...]).astype(h_ref.dtype)
        acc_ref[...] = jnp.zeros(acc_ref.shape, jnp.float32)

    u = jnp.dot(h_ref[...], wu_ref[...], preferred_element_type=jnp.float32)
    u = jnp.square(jnp.maximum(u, 0.0)).astype(wd_ref.dtype)
    acc_ref[...] += jnp.dot(u, wd_ref[...], preferred_element_type=jnp.float32)

    @pl.when(j == pl.num_programs(1) - 1)
    def _():
        o_ref[...] = x_ref[...] + _rms(acc_ref[...], gpost_ref[...])


def _mlp(x2, gpre, gpost, wu, wd, *, tm, tf):
    m, d = x2.shape
    f = wu.shape[1]
    blk = 2 * (2 * tm * d * 4 + 2 * d * tf * 2) + tm * d * 2 + tm * d * 4 + 2 * tm * tf * 4
    return pl.pallas_call(
        _mlp_kernel,
        out_shape=jax.ShapeDtypeStruct((m, d), jnp.float32),
        grid=(m // tm, f // tf),
        in_specs=[
            pl.BlockSpec((tm, d), lambda i, j: (i, 0)),
            pl.BlockSpec((1, d), lambda i, j: (0, 0)),
            pl.BlockSpec((1, d), lambda i, j: (0, 0)),
            pl.BlockSpec((d, tf), lambda i, j: (0, j)),
            pl.BlockSpec((tf, d), lambda i, j: (j, 0)),
        ],
        out_specs=pl.BlockSpec((tm, d), lambda i, j: (i, 0)),
        scratch_shapes=[pltpu.VMEM((tm, d), jnp.bfloat16), pltpu.VMEM((tm, d), jnp.float32)],
        compiler_params=pltpu.CompilerParams(
            dimension_semantics=("parallel", "arbitrary"),
            vmem_limit_bytes=_vmem_limit(blk)),
        name="mlp",
    )(x2, gpre, gpost, wu, wd)


def _rope_tables(seq):
    inv = 1.0 / (ROPE_THETA ** (jnp.arange(0, DA_HEAD_DIM, 2, dtype=jnp.float32) / DA_HEAD_DIM))
    ang = jnp.arange(seq, dtype=jnp.float32)[:, None] * inv[None, :]
    cos, sin = jnp.cos(ang), jnp.sin(ang)
    return jnp.concatenate([cos, cos], axis=1), jnp.concatenate([-sin, sin], axis=1)


def kernel(x, w_in, w_branch_a, w_branch_b, w_out, norm_mix_pre, norm_mix_post, norm_mlp_pre, norm_mlp_post,
           lam_q1, lam_k1, lam_q2, lam_k2, subln_w, na_rpb, w_up, w_down):
    b, seq, d = x.shape
    m = b * seq
    depth = w_in.shape[0]
    da_width = DA_HEADS * 2 * DA_HEAD_DIM
    na_width = NA_HEADS * NA_HEAD_DIM
    qkv_cols = 3 * da_width + 3 * na_width
    bf16 = jnp.bfloat16
    cosd, sind = _rope_tables(seq)
    row = lambda v: v.reshape(1, -1).astype(jnp.float32)
    tm = min(512, seq)
    x2 = x.reshape(m, d)
    for l in range(depth):
        lambda_init = 0.8 - 0.6 * math.exp(-0.3 * l)
        w_in_l = w_in[l].astype(bf16)
        qkv = _qkv_proj(x2, row(norm_mix_pre[l]), w_in_l, cosd, sind, seq,
                        tm=tm, tn=1024, n_cols=qkv_cols, n_rope_cols=2 * da_width)
        qkv3 = qkv.reshape(b, seq, qkv_cols)
        oa = _diff_attn(qkv3, row(lam_q1[l]), row(lam_k1[l]), row(lam_q2[l]), row(lam_k2[l]), row(subln_w[l]),
                        tq=min(512, seq), tk=min(1024, seq), lambda_init=lambda_init)
        on = _neigh_attn(qkv3, _neigh_bias(na_rpb[l], seq // GRID_W))
        x2 = _merge(x2, row(norm_mix_pre[l]), row(norm_mix_post[l]), oa.reshape(m, da_width),
                    on.reshape(m, na_width), w_in_l, w_branch_a[l].astype(bf16), w_branch_b[l].astype(bf16),
                    w_out[l].astype(bf16), tm=tm, tn=512)
        x2 = _mlp(x2, row(norm_mlp_pre[l]), row(norm_mlp_post[l]), w_up[l].astype(bf16), w_down[l].astype(bf16),
                  tm=tm, tf=1024)
    return x2.reshape(b, seq, d)
```

```python
import functools
import math

import jax
import jax.numpy as jnp
import numpy as np
from jax import lax
from jax.experimental import pallas as pl
from jax.experimental.pallas import tpu as pltpu

EPS = 1e-6
ROPE_THETA = 10000.0
GRID_W = 64
DA_HEADS = 4
DA_HEAD_DIM = 128
NA_HEADS = 8
NA_HEAD_DIM = 128
NA_KH = 8
NA_KW = 16
LANES = 128
MXU_TILE = 256
NEG_BIG = -1e30
NA_QROWS = 4
NA_KROWS = 12
V7X_VMEM_BYTES = 64 * 1024 * 1024
assert DA_HEAD_DIM == NA_HEAD_DIM
LOG2E = math.log2(math.e)
QK_MULT = DA_HEAD_DIM ** -0.5 * LOG2E

_NT = (((1,), (1,)), ((), ()))


def _vmem_limit(block_bytes):
    return int(min(block_bytes + 16 * 1024 * 1024, V7X_VMEM_BYTES - 6 * 1024 * 1024))


def _rms(x, g):
    return x * lax.rsqrt(jnp.mean(x * x, axis=-1, keepdims=True) + EPS) * g


def _qkv_kernel(x_ref, g_ref, w_ref, cos_ref, sin_ref, o_ref, h_ref, *, da_tiles, na_q_tile0, na_tiles):
    j = pl.program_id(1)

    @pl.when(j == 0)
    def _():
        h_ref[...] = _rms(x_ref[...], g_ref[...]).astype(h_ref.dtype)

    acc = jnp.dot(h_ref[...], w_ref[...], preferred_element_type=jnp.float32)

    is_da_q = j < da_tiles
    is_da_k = jnp.logical_and(j >= da_tiles, j < 2 * da_tiles)
    is_na_q = jnp.logical_and(j >= na_q_tile0, j < na_q_tile0 + na_tiles)
    rot = jnp.where(is_da_q, QK_MULT, jnp.where(is_da_k, 1.0, 0.0)).astype(jnp.float32)
    flat = jnp.where(is_na_q, QK_MULT, jnp.where(jnp.logical_or(is_da_q, is_da_k), 0.0, 1.0)).astype(jnp.float32)
    cs = cos_ref[...] * rot + flat
    sn = sin_ref[...] * rot
    for g in range(acc.shape[1] // LANES):
        y = acc[:, g * LANES:(g + 1) * LANES]
        o_ref[:, g * LANES:(g + 1) * LANES] = (y * cs + pltpu.roll(y, LANES // 2, 1) * sn).astype(o_ref.dtype)


def _qkv_proj(x2, gain, w_bf16, cosd, sind, seq, *, tm, tn, da_width, na_width):
    m, d = x2.shape
    n_cols = 3 * da_width + 3 * na_width
    pos_blocks = seq // tm
    blk = 2 * (tm * d * 4 + d * tn * 2 + tm * tn * 2 + 2 * tm * LANES * 4) + tm * d * 2 + tm * tn * 4
    return pl.pallas_call(
        functools.partial(_qkv_kernel, da_tiles=da_width // tn, na_q_tile0=3 * da_width // tn,
                          na_tiles=na_width // tn),
        out_shape=jax.ShapeDtypeStruct((m, n_cols), jnp.bfloat16),
        grid=(m // tm, n_cols // tn),
        in_specs=[
            pl.BlockSpec((tm, d), lambda i, j: (i, 0)),
            pl.BlockSpec((1, d), lambda i, j: (0, 0)),
            pl.BlockSpec((d, tn), lambda i, j: (0, j)),
            pl.BlockSpec((tm, LANES), lambda i, j: (i % pos_blocks, 0)),
            pl.BlockSpec((tm, LANES), lambda i, j: (i % pos_blocks, 0)),
        ],
        out_specs=pl.BlockSpec((tm, tn), lambda i, j: (i, j)),
        scratch_shapes=[pltpu.VMEM((tm, d), jnp.bfloat16)],
        compiler_params=pltpu.CompilerParams(
            dimension_semantics=("parallel", "arbitrary"),
            vmem_limit_bytes=_vmem_limit(blk)),
        name="qkv_proj",
    )(x2, gain, w_bf16, cosd, sind)


def _diff_attn_kernel(lq1_ref, lk1_ref, lq2_ref, lk2_ref, subw_ref, q_ref, k_ref, v_ref, o_ref,
                      m1, l1, a1, m2, l2, a2, sa1, sa2, sb1, sb2, ma1, ma2, mb1, mb2, *, tk, lambda_init):
    seq = k_ref.shape[1]
    n_chunks = seq // tk
    d = DA_HEAD_DIM
    q = q_ref[0]
    q_halves = (q[:, :d], q[:, d:])
    state = ((m1, l1, a1), (m2, l2, a2))
    for m_ref, l_ref, a_ref in state:
        m_ref[...] = jnp.full(m_ref.shape, NEG_BIG, jnp.float32)
        l_ref[...] = jnp.zeros(l_ref.shape, jnp.float32)
        a_ref[...] = jnp.zeros(a_ref.shape, jnp.float32)

    def scores(c, dst):
        for half in range(2):
            mx = None
            for t in range(tk // MXU_TILE):
                k = k_ref[0, pl.ds(c * tk + t * MXU_TILE, MXU_TILE), half * d:(half + 1) * d]
                s = lax.dot_general(q_halves[half], k, _NT, preferred_element_type=jnp.float32)
                dst[half][:, t * MXU_TILE:(t + 1) * MXU_TILE] = s
                for u in range(MXU_TILE // LANES):
                    su = s[:, u * LANES:(u + 1) * LANES]
                    mx = su if mx is None else jnp.maximum(mx, su)
            dst[2 + half][...] = jnp.broadcast_to(jnp.max(mx, axis=1, keepdims=True), mx.shape)

    def consume(c, src):
        for half, (m_ref, l_ref, a_ref) in enumerate(state):
            m_prev = m_ref[...]
            m_new = jnp.maximum(m_prev, src[2 + half][...])
            alpha = jnp.exp2(m_prev - m_new)
            m_rep = jnp.concatenate([m_new] * (MXU_TILE // LANES), axis=1)
            psum = None
            pv = None
            for t in range(tk // MXU_TILE):
                p = jnp.exp2(src[half][:, t * MXU_TILE:(t + 1) * MXU_TILE] - m_rep)
                for u in range(MXU_TILE // LANES):
                    pu = p[:, u * LANES:(u + 1) * LANES]
                    psum = pu if psum is None else psum + pu
                v = v_ref[0, pl.ds(c * tk + t * MXU_TILE, MXU_TILE), :]
                pvt = jnp.dot(p.astype(v.dtype), v, preferred_element_type=jnp.float32)
                pv = pvt if pv is None else pv + pvt
            l_ref[...] = alpha * l_ref[...] + psum
            a_ref[...] = a_ref[...] * jnp.concatenate([alpha] * (2 * d // LANES), axis=1) + pv
            m_ref[...] = m_new

    buf_a, buf_b = (sa1, sa2, ma1, ma2), (sb1, sb2, mb1, mb2)
    bufs = (buf_a, buf_b)
    scores(0, bufs[0])
    for c in range(n_chunks):
        if c + 1 < n_chunks:
            scores(c + 1, bufs[(c + 1) % 2])
        consume(c, bufs[c % 2])

    lam = (jnp.exp(jnp.sum(lq1_ref[...] * lk1_ref[...])) - jnp.exp(jnp.sum(lq2_ref[...] * lk2_ref[...]))
           + lambda_init)
    o1 = a1[...] / jnp.sum(l1[...], axis=1, keepdims=True)
    o2 = a2[...] / jnp.sum(l2[...], axis=1, keepdims=True)
    o = o1 - lam * o2
    o = _rms(o, subw_ref[...]) * (1.0 - lambda_init)
    o_ref[0] = o.astype(o_ref.dtype)


def _diff_attn(qkv3, lq1, lk1, lq2, lk2, subw, *, tq, tk, lambda_init, flags=None):
    b, seq, _ = qkv3.shape
    hd = 2 * DA_HEAD_DIM
    kern = functools.partial(_diff_attn_kernel, tk=tk, lambda_init=lambda_init)
    vec = lambda n: pl.BlockSpec((1, n), lambda bi, h, qi: (0, 0))
    blk = 2 * (2 * seq * hd * 2 + 2 * tq * hd * 2) + 2 * (2 * tq * LANES * 4 + tq * hd * 4) + 6 * tq * tk * 4
    return pl.pallas_call(
        kern,
        out_shape=jax.ShapeDtypeStruct((b, seq, DA_HEADS * hd), jnp.bfloat16),
        grid=(b, DA_HEADS, seq // tq),
        in_specs=[
            vec(DA_HEAD_DIM), vec(DA_HEAD_DIM), vec(DA_HEAD_DIM), vec(DA_HEAD_DIM), vec(hd),
            pl.BlockSpec((1, tq, hd), lambda bi, h, qi: (bi, qi, h)),
            pl.BlockSpec((1, seq, hd), lambda bi, h, qi: (bi, 0, DA_HEADS + h)),
            pl.BlockSpec((1, seq, hd), lambda bi, h, qi: (bi, 0, 2 * DA_HEADS + h)),
        ],
        out_specs=pl.BlockSpec((1, tq, hd), lambda bi, h, qi: (bi, qi, h)),
        scratch_shapes=[
            pltpu.VMEM((tq, LANES), jnp.float32), pltpu.VMEM((tq, LANES), jnp.float32),
            pltpu.VMEM((tq, hd), jnp.float32),
            pltpu.VMEM((tq, LANES), jnp.float32), pltpu.VMEM((tq, LANES), jnp.float32),
            pltpu.VMEM((tq, hd), jnp.float32),
        ] + [pltpu.VMEM((tq, tk), jnp.float32)] * 4 + [pltpu.VMEM((tq, LANES), jnp.float32)] * 4,
        compiler_params=pltpu.CompilerParams(
            dimension_semantics=("parallel", "parallel", "arbitrary"),
            vmem_limit_bytes=_vmem_limit(blk), flags=flags),
        name="diff_attn",
    )(lq1, lk1, lq2, lk2, subw, qkv3, qkv3, qkv3)


def _neigh_bias(rpb, rows):
    heads = rpb.shape[0]
    rpb = rpb.astype(jnp.float32) * LOG2E
    qc = np.arange(GRID_W)[:, None]
    kc = np.arange(GRID_W)[None, :]
    col_start = np.clip(qc - NA_KW // 2, 0, GRID_W - NA_KW)
    col_ok = (kc >= col_start) & (kc < col_start + NA_KW)
    dc = kc - qc + (NA_KW - 1)
    tcol = jnp.full((heads, 2 * NA_KH - 1, GRID_W, GRID_W), NEG_BIG, jnp.float32)
    for v in range(2 * NA_KW - 1):
        tcol = jnp.where((col_ok & (dc == v))[None, None], rpb[:, :, v][:, :, None, None], tcol)
    masked = jnp.full((heads, GRID_W, GRID_W), NEG_BIG, jnp.float32)
    nblk = rows // NA_QROWS
    tables = []
    for rb, key_row0 in ((0, 0), (1, 0), (nblk - 1, (nblk - 3) * NA_QROWS)):
        q_rows = []
        for qr in range(NA_QROWS):
            r = rb * NA_QROWS + qr
            row_start = min(max(r - NA_KH // 2, 0), rows - NA_KH)
            blocks = []
            for kr in range(NA_KROWS):
                krow = key_row0 + kr
                ok = row_start <= krow < row_start + NA_KH
                blocks.append(tcol[:, krow - r + NA_KH - 1] if ok else masked)
            q_rows.append(jnp.concatenate(blocks, axis=-1))
        tables.append(jnp.concatenate(q_rows, axis=1))
    return jnp.stack(tables)


def _neigh_attn_kernel(bias_ref, q_ref, k0_ref, k1_ref, k2_ref, v0_ref, v1_ref, v2_ref, o_ref):
    d = NA_HEAD_DIM
    nkb = q_ref.shape[1]
    for h in range(q_ref.shape[2] // d):
        cols = slice(h * d, (h + 1) * d)
        q = q_ref[0, :, cols]
        s = jnp.concatenate(
            [lax.dot_general(q, k_ref[0, :, cols], _NT, preferred_element_type=jnp.float32)
             for k_ref in (k0_ref, k1_ref, k2_ref)], axis=1)
        s = s + bias_ref[0, h]
        p = jnp.exp2(s - jnp.max(s, axis=1, keepdims=True))
        l = jnp.sum(p, axis=1, keepdims=True)
        pb = p.astype(v0_ref.dtype)
        o = None
        for t, v_ref in enumerate((v0_ref, v1_ref, v2_ref)):
            pv = jnp.dot(pb[:, t * nkb:(t + 1) * nkb], v_ref[0, :, cols], preferred_element_type=jnp.float32)
            o = pv if o is None else o + pv
        o_ref[0, :, cols] = (o / l).astype(o_ref.dtype)


def _neigh_attn(qkv3, bias):
    b, seq, _ = qkv3.shape
    width = NA_HEADS * NA_HEAD_DIM
    nq = NA_QROWS * GRID_W
    nk = NA_KROWS * GRID_W
    nblk = seq // nq
    col0 = 3 * DA_HEADS * 2 * DA_HEAD_DIM // width
    start = lambda rb: jnp.clip(rb - 1, 0, nblk - 3)
    block_class = lambda rb: jnp.where(rb == 0, 0, jnp.where(rb == nblk - 1, 2, 1))
    kv_spec = lambda base, t: pl.BlockSpec((1, nq, width), lambda bi, rb: (bi, start(rb) + t, col0 + base))
    blk = 2 * (NA_HEADS * nq * nk * 4 + 8 * nq * width * 2) + 8 * nq * nk * 4
    return pl.pallas_call(
        _neigh_attn_kernel,
        out_shape=jax.ShapeDtypeStruct((b, seq, width), jnp.bfloat16),
        grid=(b, nblk),
        in_specs=[
            pl.BlockSpec((1, NA_HEADS, nq, nk), lambda bi, rb: (block_class(rb), 0, 0, 0)),
            pl.BlockSpec((1, nq, width), lambda bi, rb: (bi, rb, col0)),
            kv_spec(1, 0), kv_spec(1, 1), kv_spec(1, 2),
            kv_spec(2, 0), kv_spec(2, 1), kv_spec(2, 2),
        ],
        out_specs=pl.BlockSpec((1, nq, width), lambda bi, rb: (bi, rb, 0)),
        compiler_params=pltpu.CompilerParams(
            dimension_semantics=("parallel", "arbitrary"),
            vmem_limit_bytes=_vmem_limit(blk)),
        name="neigh_attn",
    )(bias, qkv3, qkv3, qkv3, qkv3, qkv3, qkv3, qkv3)


def _merge_kernel(x_ref, gpre_ref, gpost_ref, oa_ref, on_ref, wga_ref, wgb_ref, wa_ref, wb_ref, wo_ref,
                  o_ref, h_ref, y_ref):
    j = pl.program_id(1)

    @pl.when(j == 0)
    def _():
        h_ref[...] = _rms(x_ref[...], gpre_ref[...]).astype(h_ref.dtype)
        y_ref[...] = jnp.zeros(y_ref.shape, jnp.float32)

    h = h_ref[...]
    f32 = jnp.float32
    ga = 1.0 / (1.0 + jnp.exp(-jnp.dot(h, wga_ref[...], preferred_element_type=f32)))
    gb = 1.0 / (1.0 + jnp.exp(-jnp.dot(h, wgb_ref[...], preferred_element_type=f32)))
    a = jnp.dot(oa_ref[...], wa_ref[...], preferred_element_type=f32)
    b = jnp.dot(on_ref[...], wb_ref[...], preferred_element_type=f32)
    mixed = (ga * a + gb * b).astype(wo_ref.dtype)
    y_ref[...] += jnp.dot(mixed, wo_ref[...], preferred_element_type=f32)

    @pl.when(j == pl.num_programs(1) - 1)
    def _():
        o_ref[...] = x_ref[...] + _rms(y_ref[...], gpost_ref[...])


def _merge(x2, gpre, gpost, oa2, on2, w_in_bf16, wa, wb, wo, *, tm, tn):
    m, d = x2.shape
    ca = oa2.shape[1]
    cb = on2.shape[1]
    gate_col0 = (w_in_bf16.shape[1] - 2 * d) // tn
    nj = d // tn
    blk = (2 * (2 * tm * d * 4 + tm * (ca + cb) * 2 + (2 * d + ca + cb + d) * tn * 2)
           + tm * d * 2 + tm * d * 4 + 6 * tm * tn * 4)
    return pl.pallas_call(
        _merge_kernel,
        out_shape=jax.ShapeDtypeStruct((m, d), jnp.float32),
        grid=(m // tm, nj),
        in_specs=[
            pl.BlockSpec((tm, d), lambda i, j: (i, 0)),
            pl.BlockSpec((1, d), lambda i, j: (0, 0)),
            pl.BlockSpec((1, d), lambda i, j: (0, 0)),
            pl.BlockSpec((tm, ca), lambda i, j: (i, 0)),
            pl.BlockSpec((tm, cb), lambda i, j: (i, 0)),
            pl.BlockSpec((d, tn), lambda i, j: (0, gate_col0 + j)),
            pl.BlockSpec((d, tn), lambda i, j: (0, gate_col0 + nj + j)),
            pl.BlockSpec((ca, tn), lambda i, j: (0, j)),
            pl.BlockSpec((cb, tn), lambda i, j: (0, j)),
            pl.BlockSpec((tn, d), lambda i, j: (j, 0)),
        ],
        out_specs=pl.BlockSpec((tm, d), lambda i, j: (i, 0)),
        scratch_shapes=[pltpu.VMEM((tm, d), jnp.bfloat16), pltpu.VMEM((tm, d), jnp.float32)],
        compiler_params=pltpu.CompilerParams(
            dimension_semantics=("parallel", "arbitrary"),
            vmem_limit_bytes=_vmem_limit(blk)),
        name="merge",
    )(x2, gpre, gpost, oa2, on2, w_in_bf16, w_in_bf16, wa, wb, wo)


def _mlp_kernel(x_ref, gpre_ref, gpost_ref, wu_ref, wd_ref, o_ref, h_ref, acc_ref):
    j = pl.program_id(1)

    @pl.when(j == 0)
    def _():
        h_ref[...] = _rms(x_ref[...], gpre_ref[...]).astype(h_ref.dtype)
        acc_ref[...] = jnp.zeros(acc_ref.shape, jnp.float32)

    u = jnp.dot(h_ref[...], wu_ref[...], preferred_element_type=jnp.float32)
    u = jnp.square(jnp.maximum(u, 0.0)).astype(wd_ref.dtype)
    acc_ref[...] += jnp.dot(u, wd_ref[...], preferred_element_type=jnp.float32)

    @pl.when(j == pl.num_programs(1) - 1)
    def _():
        o_ref[...] = x_ref[...] + _rms(acc_ref[...], gpost_ref[...])


def _mlp(x2, gpre, gpost, wu, wd, *, tm, tf):
    m, d = x2.shape
    f = wu.shape[1]
    blk = 2 * (2 * tm * d * 4 + 2 * d * tf * 2) + tm * d * 2 + tm * d * 4 + 2 * tm * tf * 4
    return pl.pallas_call(
        _mlp_kernel,
        out_shape=jax.ShapeDtypeStruct((m, d), jnp.float32),
        grid=(m // tm, f // tf),
        in_specs=[
            pl.BlockSpec((tm, d), lambda i, j: (i, 0)),
            pl.BlockSpec((1, d), lambda i, j: (0, 0)),
            pl.BlockSpec((1, d), lambda i, j: (0, 0)),
            pl.BlockSpec((d, tf), lambda i, j: (0, j)),
            pl.BlockSpec((tf, d), lambda i, j: (j, 0)),
        ],
        out_specs=pl.BlockSpec((tm, d), lambda i, j: (i, 0)),
        scratch_shapes=[pltpu.VMEM((tm, d), jnp.bfloat16), pltpu.VMEM((tm, d), jnp.float32)],
        compiler_params=pltpu.CompilerParams(
            dimension_semantics=("parallel", "arbitrary"),
            vmem_limit_bytes=_vmem_limit(blk)),
        name="mlp",
    )(x2, gpre, gpost, wu, wd)


def _rope_tables(seq):
    inv = 1.0 / (ROPE_THETA ** (jnp.arange(0, DA_HEAD_DIM, 2, dtype=jnp.float32) / DA_HEAD_DIM))
    ang = jnp.arange(seq, dtype=jnp.float32)[:, None] * inv[None, :]
    cos, sin = jnp.cos(ang), jnp.sin(ang)
    return jnp.concatenate([cos, cos], axis=1), jnp.concatenate([-sin, sin], axis=1)


def kernel(x, w_in, w_branch_a, w_branch_b, w_out, norm_mix_pre, norm_mix_post, norm_mlp_pre, norm_mlp_post,
           lam_q1, lam_k1, lam_q2, lam_k2, subln_w, na_rpb, w_up, w_down):
    b, seq, d = x.shape
    m = b * seq
    depth = w_in.shape[0]
    da_width = DA_HEADS * 2 * DA_HEAD_DIM
    na_width = NA_HEADS * NA_HEAD_DIM
    qkv_cols = 3 * da_width + 3 * na_width
    bf16 = jnp.bfloat16
    cosd, sind = _rope_tables(seq)
    row = lambda v: v.reshape(1, -1).astype(jnp.float32)
    tm = min(512, seq)
    x2 = x.reshape(m, d)
    for l in range(depth):
        lambda_init = 0.8 - 0.6 * math.exp(-0.3 * l)
        w_in_l = w_in[l].astype(bf16)
        qkv = _qkv_proj(x2, row(norm_mix_pre[l]), w_in_l, cosd, sind, seq,
                        tm=min(1024, seq), tn=1024, da_width=da_width, na_width=na_width)
        qkv3 = qkv.reshape(b, seq, qkv_cols)
        oa = _diff_attn(qkv3, row(lam_q1[l]), row(lam_k1[l]), row(lam_q2[l]), row(lam_k2[l]), row(subln_w[l]),
                        tq=min(512, seq), tk=min(1024, seq), lambda_init=lambda_init)
        on = _neigh_attn(qkv3, _neigh_bias(na_rpb[l], seq // GRID_W))
        x2 = _merge(x2, row(norm_mix_pre[l]), row(norm_mix_post[l]), oa.reshape(m, da_width),
                    on.reshape(m, na_width), w_in_l, w_branch_a[l].astype(bf16), w_branch_b[l].astype(bf16),
                    w_out[l].astype(bf16), tm=tm, tn=512)
        x2 = _mlp(x2, row(norm_mlp_pre[l]), row(norm_mlp_post[l]), w_up[l].astype(bf16), w_down[l].astype(bf16),
                  tm=tm, tf=1024)
    return x2.reshape(b, seq, d)
```

```python
import functools
import math

import jax
import jax.numpy as jnp
import numpy as np
from jax import lax
from jax.experimental import pallas as pl
from jax.experimental.pallas import tpu as pltpu

EPS = 1e-6
ROPE_THETA = 10000.0
GRID_W = 64
DA_HEADS = 4
DA_HEAD_DIM = 128
NA_HEADS = 8
NA_HEAD_DIM = 128
NA_KH = 8
NA_KW = 16
LANES = 128
SUBLANES = 8
MXU_TILE = 256
NEG_BIG = -1e30
NA_QROWS = 4
NA_KROWS = 12
V7X_VMEM_BYTES = 64 * 1024 * 1024
assert DA_HEAD_DIM == NA_HEAD_DIM
LOG2E = math.log2(math.e)
QK_MULT = DA_HEAD_DIM ** -0.5 * LOG2E

_NT = (((1,), (1,)), ((), ()))


def _vmem_limit(block_bytes):
    return int(min(block_bytes + 16 * 1024 * 1024, V7X_VMEM_BYTES - 6 * 1024 * 1024))


def _rms(x, g):
    return x * lax.rsqrt(jnp.mean(x * x, axis=-1, keepdims=True) + EPS) * g


def _qkv_kernel(x_ref, g_ref, w_ref, cos_ref, sin_ref, o_ref, h_ref, *, da_tiles, na_q_tile0, na_tiles):
    j = pl.program_id(1)

    @pl.when(j == 0)
    def _():
        h_ref[...] = _rms(x_ref[...], g_ref[...]).astype(h_ref.dtype)

    acc = jnp.dot(h_ref[...], w_ref[...], preferred_element_type=jnp.float32)

    is_da_q = j < da_tiles
    is_da_k = jnp.logical_and(j >= da_tiles, j < 2 * da_tiles)
    is_na_q = jnp.logical_and(j >= na_q_tile0, j < na_q_tile0 + na_tiles)
    rot = jnp.where(is_da_q, QK_MULT, jnp.where(is_da_k, 1.0, 0.0)).astype(jnp.float32)
    flat = jnp.where(is_na_q, QK_MULT, jnp.where(jnp.logical_or(is_da_q, is_da_k), 0.0, 1.0)).astype(jnp.float32)
    cs = cos_ref[...] * rot + flat
    sn = sin_ref[...] * rot
    for g in range(acc.shape[1] // LANES):
        y = acc[:, g * LANES:(g + 1) * LANES]
        o_ref[:, g * LANES:(g + 1) * LANES] = (y * cs + pltpu.roll(y, LANES // 2, 1) * sn).astype(o_ref.dtype)


def _qkv_proj(x2, gain, w_bf16, cosd, sind, seq, *, tm, tn, da_width, na_width):
    m, d = x2.shape
    n_cols = 3 * da_width + 3 * na_width
    pos_blocks = seq // tm
    blk = 2 * (tm * d * 4 + d * tn * 2 + tm * tn * 2 + 2 * tm * LANES * 4) + tm * d * 2 + tm * tn * 4
    return pl.pallas_call(
        functools.partial(_qkv_kernel, da_tiles=da_width // tn, na_q_tile0=3 * da_width // tn,
                          na_tiles=na_width // tn),
        out_shape=jax.ShapeDtypeStruct((m, n_cols), jnp.bfloat16),
        grid=(m // tm, n_cols // tn),
        in_specs=[
            pl.BlockSpec((tm, d), lambda i, j: (i, 0)),
            pl.BlockSpec((1, d), lambda i, j: (0, 0)),
            pl.BlockSpec((d, tn), lambda i, j: (0, j)),
            pl.BlockSpec((tm, LANES), lambda i, j: (i % pos_blocks, 0)),
            pl.BlockSpec((tm, LANES), lambda i, j: (i % pos_blocks, 0)),
        ],
        out_specs=pl.BlockSpec((tm, tn), lambda i, j: (i, j)),
        scratch_shapes=[pltpu.VMEM((tm, d), jnp.bfloat16)],
        compiler_params=pltpu.CompilerParams(
            dimension_semantics=("parallel", "arbitrary"),
            vmem_limit_bytes=_vmem_limit(blk)),
        name="qkv_proj",
    )(x2, gain, w_bf16, cosd, sind)


def _diff_attn_kernel(lq1_ref, lk1_ref, lq2_ref, lk2_ref, subw_ref, q_ref, k_ref, v_ref, o_ref,
                      m_ref, l_ref, a_ref, sa1, sa2, sb1, sb2, ma1, ma2, mb1, mb2, *, tq, tk, lambda_init):
    seq = k_ref.shape[1]
    n_chunks = seq // tk
    n_sub = q_ref.shape[1] // tq
    d = DA_HEAD_DIM
    lam = (jnp.exp(jnp.sum(lq1_ref[...] * lk1_ref[...])) - jnp.exp(jnp.sum(lq2_ref[...] * lk2_ref[...]))
           + lambda_init)

    def scores(item, dst, token):
        sub, c = item
        for half in range(2):
            q = q_ref[0, sub * tq:(sub + 1) * tq, half * d:(half + 1) * d]
            mx = None
            for t in range(tk // MXU_TILE):
                k = k_ref[0, pl.ds(c * tk + t * MXU_TILE, MXU_TILE), half * d:(half + 1) * d]
                if token is not None:
                    k = pltpu.bitcast(pltpu.bitcast(k, jnp.uint32) | token, k.dtype)
                s = lax.dot_general(q, k, _NT, preferred_element_type=jnp.float32)
                dst[half][:, t * MXU_TILE:(t + 1) * MXU_TILE] = s
                for u in range(MXU_TILE // LANES):
                    su = s[:, u * LANES:(u + 1) * LANES]
                    mx = su if mx is None else jnp.maximum(mx, su)
            dst[2 + half][...] = jnp.broadcast_to(jnp.max(mx, axis=1, keepdims=True), mx.shape)

    def consume(item, src):
        sub, c = item
        token = None
        for half in range(2):
            if c == 0:
                m_prev = jnp.full((tq, LANES), NEG_BIG, jnp.float32)
            else:
                m_prev = m_ref[sub, half]
            m_new = jnp.maximum(m_prev, src[2 + half][...])
            alpha = jnp.exp2(m_prev - m_new)
            m_rep = jnp.concatenate([m_new] * (MXU_TILE // LANES), axis=1)
            psum = None
            pv = None
            for t in range(tk // MXU_TILE):
                p = jnp.exp2(src[half][:, t * MXU_TILE:(t + 1) * MXU_TILE] - m_rep)
                for u in range(MXU_TILE // LANES):
                    pu = p[:, u * LANES:(u + 1) * LANES]
                    psum = pu if psum is None else psum + pu
                v = v_ref[0, pl.ds(c * tk + t * MXU_TILE, MXU_TILE), :]
                pvt = jnp.dot(p.astype(v.dtype), v, preferred_element_type=jnp.float32)
                pv = pvt if pv is None else pv + pvt
            if c == 0:
                l_ref[sub, half] = psum
                a_ref[sub, half] = pv
            else:
                l_ref[sub, half] = alpha * l_ref[sub, half] + psum
                a_ref[sub, half] = a_ref[sub, half] * jnp.concatenate([alpha] * (2 * d // LANES), axis=1) + pv
            m_ref[sub, half] = m_new
            if half == 0:
                bits = pltpu.bitcast(pv[0:SUBLANES, 0:LANES], jnp.uint32)
                token = ((bits >> 16) >> 16)[0:1, :]
        return token

    def finalize(sub):
        o1 = a_ref[sub, 0] / jnp.sum(l_ref[sub, 0], axis=1, keepdims=True)
        o2 = a_ref[sub, 1] / jnp.sum(l_ref[sub, 1], axis=1, keepdims=True)
        o = _rms(o1 - lam * o2, subw_ref[...]) * (1.0 - lambda_init)
        o_ref[0, sub * tq:(sub + 1) * tq, :] = o.astype(o_ref.dtype)

    bufs = ((sa1, sa2, ma1, ma2), (sb1, sb2, mb1, mb2))
    items = [(sub, c) for sub in range(n_sub) for c in range(n_chunks)]
    scores(items[0], bufs[0], None)
    token = None
    for n, item in enumerate(items):
        if n + 1 < len(items):
            scores(items[n + 1], bufs[(n + 1) % 2], token)
        token = consume(item, bufs[n % 2])
        if item[1] == n_chunks - 1:
            finalize(item[0])


def _diff_attn(qkv3, lq1, lk1, lq2, lk2, subw, *, tq, tk, n_sub, lambda_init):
    b, seq, _ = qkv3.shape
    hd = 2 * DA_HEAD_DIM
    tb = n_sub * tq
    kern = functools.partial(_diff_attn_kernel, tq=tq, tk=tk, lambda_init=lambda_init)
    vec = lambda n: pl.BlockSpec((1, n), lambda bi, h, qi: (0, 0))
    blk = (2 * (2 * seq * hd * 2 + 2 * tb * hd * 2) + 2 * n_sub * (2 * tq * LANES * 4 + tq * hd * 4)
           + 6 * tq * tk * 4)
    return pl.pallas_call(
        kern,
        out_shape=jax.ShapeDtypeStruct((b, seq, DA_HEADS * hd), jnp.bfloat16),
        grid=(b, DA_HEADS, seq // tb),
        in_specs=[
            vec(DA_HEAD_DIM), vec(DA_HEAD_DIM), vec(DA_HEAD_DIM), vec(DA_HEAD_DIM), vec(hd),
            pl.BlockSpec((1, tb, hd), lambda bi, h, qi: (bi, qi, h)),
            pl.BlockSpec((1, seq, hd), lambda bi, h, qi: (bi, 0, DA_HEADS + h)),
            pl.BlockSpec((1, seq, hd), lambda bi, h, qi: (bi, 0, 2 * DA_HEADS + h)),
        ],
        out_specs=pl.BlockSpec((1, tb, hd), lambda bi, h, qi: (bi, qi, h)),
        scratch_shapes=[
            pltpu.VMEM((n_sub, 2, tq, LANES), jnp.float32),
            pltpu.VMEM((n_sub, 2, tq, LANES), jnp.float32),
            pltpu.VMEM((n_sub, 2, tq, hd), jnp.float32),
        ] + [pltpu.VMEM((tq, tk), jnp.float32)] * 4 + [pltpu.VMEM((tq, LANES), jnp.float32)] * 4,
        compiler_params=pltpu.CompilerParams(
            dimension_semantics=("parallel", "parallel", "arbitrary"),
            vmem_limit_bytes=_vmem_limit(blk)),
        name="diff_attn",
    )(lq1, lk1, lq2, lk2, subw, qkv3, qkv3, qkv3)


def _neigh_bias(rpb, rows):
    heads = rpb.shape[0]
    rpb = rpb.astype(jnp.float32) * LOG2E
    qc = np.arange(GRID_W)[:, None]
    kc = np.arange(GRID_W)[None, :]
    col_start = np.clip(qc - NA_KW // 2, 0, GRID_W - NA_KW)
    col_ok = (kc >= col_start) & (kc < col_start + NA_KW)
    dc = kc - qc + (NA_KW - 1)
    tcol = jnp.full((heads, 2 * NA_KH - 1, GRID_W, GRID_W), NEG_BIG, jnp.float32)
    for v in range(2 * NA_KW - 1):
        tcol = jnp.where((col_ok & (dc == v))[None, None], rpb[:, :, v][:, :, None, None], tcol)
    masked = jnp.full((heads, GRID_W, GRID_W), NEG_BIG, jnp.float32)
    nblk = rows // NA_QROWS
    tables = []
    for rb, key_row0 in ((0, 0), (1, 0), (nblk - 1, (nblk - 3) * NA_QROWS)):
        q_rows = []
        for qr in range(NA_QROWS):
            r = rb * NA_QROWS + qr
            row_start = min(max(r - NA_KH // 2, 0), rows - NA_KH)
            blocks = []
            for kr in range(NA_KROWS):
                krow = key_row0 + kr
                ok = row_start <= krow < row_start + NA_KH
                blocks.append(tcol[:, krow - r + NA_KH - 1] if ok else masked)
            q_rows.append(jnp.concatenate(blocks, axis=-1))
        tables.append(jnp.concatenate(q_rows, axis=1))
    return jnp.stack(tables)


def _neigh_attn_kernel(bias_ref, q_ref, k0_ref, k1_ref, k2_ref, v0_ref, v1_ref, v2_ref, o_ref):
    d = NA_HEAD_DIM
    nkb = q_ref.shape[1]
    for h in range(q_ref.shape[2] // d):
        cols = slice(h * d, (h + 1) * d)
        q = q_ref[0, :, cols]
        s = jnp.concatenate(
            [lax.dot_general(q, k_ref[0, :, cols], _NT, preferred_element_type=jnp.float32)
             for k_ref in (k0_ref, k1_ref, k2_ref)], axis=1)
        s = s + bias_ref[0, h]
        p = jnp.exp2(s - jnp.max(s, axis=1, keepdims=True))
        l = jnp.sum(p, axis=1, keepdims=True)
        pb = p.astype(v0_ref.dtype)
        o = None
        for t, v_ref in enumerate((v0_ref, v1_ref, v2_ref)):
            pv = jnp.dot(pb[:, t * nkb:(t + 1) * nkb], v_ref[0, :, cols], preferred_element_type=jnp.float32)
            o = pv if o is None else o + pv
        o_ref[0, :, cols] = (o / l).astype(o_ref.dtype)


def _neigh_attn(qkv3, bias):
    b, seq, _ = qkv3.shape
    width = NA_HEADS * NA_HEAD_DIM
    nq = NA_QROWS * GRID_W
    nk = NA_KROWS * GRID_W
    nblk = seq // nq
    col0 = 3 * DA_HEADS * 2 * DA_HEAD_DIM // width
    start = lambda rb: jnp.clip(rb - 1, 0, nblk - 3)
    block_class = lambda rb: jnp.where(rb == 0, 0, jnp.where(rb == nblk - 1, 2, 1))
    kv_spec = lambda base, t: pl.BlockSpec((1, nq, width), lambda bi, rb: (bi, start(rb) + t, col0 + base))
    blk = 2 * (NA_HEADS * nq * nk * 4 + 8 * nq * width * 2) + 8 * nq * nk * 4
    return pl.pallas_call(
        _neigh_attn_kernel,
        out_shape=jax.ShapeDtypeStruct((b, seq, width), jnp.bfloat16),
        grid=(b, nblk),
        in_specs=[
            pl.BlockSpec((1, NA_HEADS, nq, nk), lambda bi, rb: (block_class(rb), 0, 0, 0)),
            pl.BlockSpec((1, nq, width), lambda bi, rb: (bi, rb, col0)),
            kv_spec(1, 0), kv_spec(1, 1), kv_spec(1, 2),
            kv_spec(2, 0), kv_spec(2, 1), kv_spec(2, 2),
        ],
        out_specs=pl.BlockSpec((1, nq, width), lambda bi, rb: (bi, rb, 0)),
        compiler_params=pltpu.CompilerParams(
            dimension_semantics=("parallel", "arbitrary"),
            vmem_limit_bytes=_vmem_limit(blk)),
        name="neigh_attn",
    )(bias, qkv3, qkv3, qkv3, qkv3, qkv3, qkv3, qkv3)


def _merge_kernel(x_ref, gpre_ref, gpost_ref, oa_ref, on_ref, wga_ref, wgb_ref, wa_ref, wb_ref, wo_ref,
                  o_ref, h_ref, y_ref):
    j = pl.program_id(1)

    @pl.when(j == 0)
    def _():
        h_ref[...] = _rms(x_ref[...], gpre_ref[...]).astype(h_ref.dtype)
        y_ref[...] = jnp.zeros(y_ref.shape, jnp.float32)

    h = h_ref[...]
    f32 = jnp.float32
    ga = 1.0 / (1.0 + jnp.exp(-jnp.dot(h, wga_ref[...], preferred_element_type=f32)))
    gb = 1.0 / (1.0 + jnp.exp(-jnp.dot(h, wgb_ref[...], preferred_element_type=f32)))
    a = jnp.dot(oa_ref[...], wa_ref[...], preferred_element_type=f32)
    b = jnp.dot(on_ref[...], wb_ref[...], preferred_element_type=f32)
    mixed = (ga * a + gb * b).astype(wo_ref.dtype)
    y_ref[...] += jnp.dot(mixed, wo_ref[...], preferred_element_type=f32)

    @pl.when(j == pl.num_programs(1) - 1)
    def _():
        o_ref[...] = x_ref[...] + _rms(y_ref[...], gpost_ref[...])


def _merge(x2, gpre, gpost, oa2, on2, w_in_bf16, wa, wb, wo, *, tm, tn):
    m, d = x2.shape
    ca = oa2.shape[1]
    cb = on2.shape[1]
    gate_col0 = (w_in_bf16.shape[1] - 2 * d) // tn
    nj = d // tn
    blk = (2 * (2 * tm * d * 4 + tm * (ca + cb) * 2 + (2 * d + ca + cb + d) * tn * 2)
           + tm * d * 2 + tm * d * 4 + 6 * tm * tn * 4)
    return pl.pallas_call(
        _merge_kernel,
        out_shape=jax.ShapeDtypeStruct((m, d), jnp.float32),
        grid=(m // tm, nj),
        in_specs=[
            pl.BlockSpec((tm, d), lambda i, j: (i, 0)),
            pl.BlockSpec((1, d), lambda i, j: (0, 0)),
            pl.BlockSpec((1, d), lambda i, j: (0, 0)),
            pl.BlockSpec((tm, ca), lambda i, j: (i, 0)),
            pl.BlockSpec((tm, cb), lambda i, j: (i, 0)),
            pl.BlockSpec((d, tn), lambda i, j: (0, gate_col0 + j)),
            pl.BlockSpec((d, tn), lambda i, j: (0, gate_col0 + nj + j)),
            pl.BlockSpec((ca, tn), lambda i, j: (0, j)),
            pl.BlockSpec((cb, tn), lambda i, j: (0, j)),
            pl.BlockSpec((tn, d), lambda i, j: (j, 0)),
        ],
        out_specs=pl.BlockSpec((tm, d), lambda i, j: (i, 0)),
        scratch_shapes=[pltpu.VMEM((tm, d), jnp.bfloat16), pltpu.VMEM((tm, d), jnp.float32)],
        compiler_params=pltpu.CompilerParams(
            dimension_semantics=("parallel", "arbitrary"),
            vmem_limit_bytes=_vmem_limit(blk)),
        name="merge",
    )(x2, gpre, gpost, oa2, on2, w_in_bf16, w_in_bf16, wa, wb, wo)


def _mlp_kernel(x_ref, gpre_ref, gpost_ref, wu_ref, wd_ref, o_ref, h_ref, acc_ref):
    j = pl.program_id(1)

    @pl.when(j == 0)
    def _():
        h_ref[...] = _rms(x_ref[...], gpre_ref[...]).astype(h_ref.dtype)
        acc_ref[...] = jnp.zeros(acc_ref.shape, jnp.float32)

    u = jnp.dot(h_ref[...], wu_ref[...], preferred_element_type=jnp.float32)
    u = jnp.square(jnp.maximum(u, 0.0)).astype(wd_ref.dtype)
    acc_ref[...] += jnp.dot(u, wd_ref[...], preferred_element_type=jnp.float32)

    @pl.when(j == pl.num_programs(1) - 1)
    def _():
        o_ref[...] = x_ref[...] + _rms(acc_ref[...], gpost_ref[...])


def _mlp(x2, gpre, gpost, wu, wd, *, tm, tf):
    m, d = x2.shape
    f = wu.shape[1]
    blk = 2 * (2 * tm * d * 4 + 2 * d * tf * 2) + tm * d * 2 + tm * d * 4 + 2 * tm * tf * 4
    return pl.pallas_call(
        _mlp_kernel,
        out_shape=jax.ShapeDtypeStruct((m, d), jnp.float32),
        grid=(m // tm, f // tf),
        in_specs=[
            pl.BlockSpec((tm, d), lambda i, j: (i, 0)),
            pl.BlockSpec((1, d), lambda i, j: (0, 0)),
            pl.BlockSpec((1, d), lambda i, j: (0, 0)),
            pl.BlockSpec((d, tf), lambda i, j: (0, j)),
            pl.BlockSpec((tf, d), lambda i, j: (j, 0)),
        ],
        out_specs=pl.BlockSpec((tm, d), lambda i, j: (i, 0)),
        scratch_shapes=[pltpu.VMEM((tm, d), jnp.bfloat16), pltpu.VMEM((tm, d), jnp.float32)],
        compiler_params=pltpu.CompilerParams(
            dimension_semantics=("parallel", "arbitrary"),
            vmem_limit_bytes=_vmem_limit(blk)),
        name="mlp",
    )(x2, gpre, gpost, wu, wd)


def _rope_tables(seq):
    inv = 1.0 / (ROPE_THETA ** (jnp.arange(0, DA_HEAD_DIM, 2, dtype=jnp.float32) / DA_HEAD_DIM))
    ang = jnp.arange(seq, dtype=jnp.float32)[:, None] * inv[None, :]
    cos, sin = jnp.cos(ang), jnp.sin(ang)
    return jnp.concatenate([cos, cos], axis=1), jnp.concatenate([-sin, sin], axis=1)


def kernel(x, w_in, w_branch_a, w_branch_b, w_out, norm_mix_pre, norm_mix_post, norm_mlp_pre, norm_mlp_post,
           lam_q1, lam_k1, lam_q2, lam_k2, subln_w, na_rpb, w_up, w_down):
    b, seq, d = x.shape
    m = b * seq
    depth = w_in.shape[0]
    da_width = DA_HEADS * 2 * DA_HEAD_DIM
    na_width = NA_HEADS * NA_HEAD_DIM
    qkv_cols = 3 * da_width + 3 * na_width
    bf16 = jnp.bfloat16
    cosd, sind = _rope_tables(seq)
    row = lambda v: v.reshape(1, -1).astype(jnp.float32)
    tm = min(512, seq)
    tq = min(512, seq)
    x2 = x.reshape(m, d)
    for l in range(depth):
        lambda_init = 0.8 - 0.6 * math.exp(-0.3 * l)
        w_in_l = w_in[l].astype(bf16)
        qkv = _qkv_proj(x2, row(norm_mix_pre[l]), w_in_l, cosd, sind, seq,
                        tm=min(1024, seq), tn=1024, da_width=da_width, na_width=na_width)
        qkv3 = qkv.reshape(b, seq, qkv_cols)
        oa = _diff_attn(qkv3, row(lam_q1[l]), row(lam_k1[l]), row(lam_q2[l]), row(lam_k2[l]), row(subln_w[l]),
                        tq=tq, tk=min(1024, seq), n_sub=min(2, seq // tq), lambda_init=lambda_init)
        on = _neigh_attn(qkv3, _neigh_bias(na_rpb[l], seq // GRID_W))
        x2 = _merge(x2, row(norm_mix_pre[l]), row(norm_mix_post[l]), oa.reshape(m, da_width),
                    on.reshape(m, na_width), w_in_l, w_branch_a[l].astype(bf16), w_branch_b[l].astype(bf16),
                    w_out[l].astype(bf16), tm=tm, tn=512)
        x2 = _mlp(x2, row(norm_mlp_pre[l]), row(norm_mlp_post[l]), w_up[l].astype(bf16), w_down[l].astype(bf16),
                  tm=tm, tf=1024)
    return x2.reshape(b, seq, d)
```

```python
import functools
import math

import jax
import jax.numpy as jnp
import numpy as np
from jax import lax
from jax.experimental import pallas as pl
from jax.experimental.pallas import tpu as pltpu

EPS = 1e-6
ROPE_THETA = 10000.0
GRID_W = 64
DA_HEADS = 4
DA_HEAD_DIM = 128
NA_HEADS = 8
NA_HEAD_DIM = 128
NA_KH = 8
NA_KW = 16
LANES = 128
SUBLANES = 8
MXU_TILE = 256
NEG_BIG = -1e30
NA_QROWS = 4
NA_KROWS = 12
V7X_VMEM_BYTES = 64 * 1024 * 1024
assert DA_HEAD_DIM == NA_HEAD_DIM
LOG2E = math.log2(math.e)
QK_MULT = DA_HEAD_DIM ** -0.5 * LOG2E

_NT = (((1,), (1,)), ((), ()))


def _vmem_limit(block_bytes):
    return int(min(block_bytes + 16 * 1024 * 1024, V7X_VMEM_BYTES - 6 * 1024 * 1024))


def _rms(x, g):
    return x * lax.rsqrt(jnp.mean(x * x, axis=-1, keepdims=True) + EPS) * g


def _zero_after(x, dtype):
    bits = pltpu.bitcast(x[0:SUBLANES, 0:LANES], jnp.uint32)
    return pltpu.bitcast((bits >> 16) >> 16, jnp.float32)[0:1, :].astype(dtype)


def _qkv_kernel(x_ref, g_ref, w_ref, cos_ref, sin_ref, o_ref, h_ref, *, group_cols):
    j = pl.program_id(1)

    @pl.when(j == 0)
    def _():
        h_ref[...] = _rms(x_ref[...], g_ref[...]).astype(h_ref.dtype)

    acc = jnp.dot(h_ref[...], w_ref[...], preferred_element_type=jnp.float32)

    groups_per_tile = acc.shape[1] // group_cols
    for gi in range(groups_per_tile):
        group = j * groups_per_tile + gi
        rot = jnp.where(group == 0, QK_MULT, jnp.where(group == 1, 1.0, 0.0)).astype(jnp.float32)
        flat = jnp.where(group == 3, QK_MULT, jnp.where(group <= 1, 0.0, 1.0)).astype(jnp.float32)
        cs = cos_ref[...] * rot + flat
        sn = sin_ref[...] * rot
        for g in range(gi * group_cols // LANES, (gi + 1) * group_cols // LANES):
            y = acc[:, g * LANES:(g + 1) * LANES]
            o_ref[:, g * LANES:(g + 1) * LANES] = (y * cs + pltpu.roll(y, LANES // 2, 1) * sn).astype(o_ref.dtype)


def _qkv_proj(x2, gain, w_bf16, cosd, sind, seq, *, tm, tn, da_width, na_width):
    m, d = x2.shape
    assert da_width == na_width and tn % da_width == 0
    n_cols = 3 * da_width + 3 * na_width
    pos_blocks = seq // tm
    blk = 2 * (tm * d * 4 + d * tn * 2 + tm * tn * 2 + 2 * tm * LANES * 4) + tm * d * 2 + tm * tn * 4
    return pl.pallas_call(
        functools.partial(_qkv_kernel, group_cols=da_width),
        out_shape=jax.ShapeDtypeStruct((m, n_cols), jnp.bfloat16),
        grid=(m // tm, n_cols // tn),
        in_specs=[
            pl.BlockSpec((tm, d), lambda i, j: (i, 0)),
            pl.BlockSpec((1, d), lambda i, j: (0, 0)),
            pl.BlockSpec((d, tn), lambda i, j: (0, j)),
            pl.BlockSpec((tm, LANES), lambda i, j: (i % pos_blocks, 0)),
            pl.BlockSpec((tm, LANES), lambda i, j: (i % pos_blocks, 0)),
        ],
        out_specs=pl.BlockSpec((tm, tn), lambda i, j: (i, j)),
        scratch_shapes=[pltpu.VMEM((tm, d), jnp.bfloat16)],
        compiler_params=pltpu.CompilerParams(
            dimension_semantics=("parallel", "arbitrary"),
            vmem_limit_bytes=_vmem_limit(blk)),
        name="qkv_proj",
    )(x2, gain, w_bf16, cosd, sind)


def _diff_attn_kernel(lq1_ref, lk1_ref, lq2_ref, lk2_ref, subw_ref, q_ref, k_ref, v_ref, o_ref,
                      m_ref, l_ref, a_ref, sa1, sa2, sb1, sb2, ma1, ma2, mb1, mb2, *, tq, tk, lambda_init):
    seq = k_ref.shape[1]
    n_chunks = seq // tk
    n_sub = q_ref.shape[1] // tq
    d = DA_HEAD_DIM
    lam = (jnp.exp(jnp.sum(lq1_ref[...] * lk1_ref[...])) - jnp.exp(jnp.sum(lq2_ref[...] * lk2_ref[...]))
           + lambda_init)

    def scores(item, dst, token):
        sub, c = item
        for half in range(2):
            q = q_ref[0, sub * tq:(sub + 1) * tq, half * d:(half + 1) * d]
            mx = None
            for t in range(tk // MXU_TILE):
                k = k_ref[0, pl.ds(c * tk + t * MXU_TILE, MXU_TILE), half * d:(half + 1) * d]
                if token is not None:
                    k = k + token
                s = lax.dot_general(q, k, _NT, preferred_element_type=jnp.float32)
                dst[half][:, t * MXU_TILE:(t + 1) * MXU_TILE] = s
                for u in range(MXU_TILE // LANES):
                    su = s[:, u * LANES:(u + 1) * LANES]
                    mx = su if mx is None else jnp.maximum(mx, su)
            dst[2 + half][...] = jnp.broadcast_to(jnp.max(mx, axis=1, keepdims=True), mx.shape)

    def consume(item, src):
        sub, c = item
        token = None
        for half in range(2):
            if c == 0:
                m_prev = jnp.full((tq, LANES), NEG_BIG, jnp.float32)
            else:
                m_prev = m_ref[sub, half]
            m_new = jnp.maximum(m_prev, src[2 + half][...])
            alpha = jnp.exp2(m_prev - m_new)
            m_rep = jnp.concatenate([m_new] * (MXU_TILE // LANES), axis=1)
            psum = None
            pv = None
            for t in range(tk // MXU_TILE):
                p = jnp.exp2(src[half][:, t * MXU_TILE:(t + 1) * MXU_TILE] - m_rep)
                for u in range(MXU_TILE // LANES):
                    pu = p[:, u * LANES:(u + 1) * LANES]
                    psum = pu if psum is None else psum + pu
                v = v_ref[0, pl.ds(c * tk + t * MXU_TILE, MXU_TILE), :]
                pvt = jnp.dot(p.astype(v.dtype), v, preferred_element_type=jnp.float32)
                pv = pvt if pv is None else pv + pvt
            if c == 0:
                l_ref[sub, half] = psum
                a_ref[sub, half] = pv
            else:
                l_ref[sub, half] = alpha * l_ref[sub, half] + psum
                a_ref[sub, half] = a_ref[sub, half] * jnp.concatenate([alpha] * (2 * d // LANES), axis=1) + pv
            m_ref[sub, half] = m_new
            if half == 0:
                token = _zero_after(pv, k_ref.dtype)
        return token

    def finalize(sub):
        o1 = a_ref[sub, 0] / jnp.sum(l_ref[sub, 0], axis=1, keepdims=True)
        o2 = a_ref[sub, 1] / jnp.sum(l_ref[sub, 1], axis=1, keepdims=True)
        o = _rms(o1 - lam * o2, subw_ref[...]) * (1.0 - lambda_init)
        o_ref[0, sub * tq:(sub + 1) * tq, :] = o.astype(o_ref.dtype)

    bufs = ((sa1, sa2, ma1, ma2), (sb1, sb2, mb1, mb2))
    items = [(sub, c) for sub in range(n_sub) for c in range(n_chunks)]
    scores(items[0], bufs[0], None)
    token = None
    for n, item in enumerate(items):
        if n + 1 < len(items):
            scores(items[n + 1], bufs[(n + 1) % 2], token)
        token = consume(item, bufs[n % 2])
        if item[1] == n_chunks - 1:
            finalize(item[0])


def _diff_attn(qkv3, lq1, lk1, lq2, lk2, subw, *, tq, tk, n_sub, lambda_init):
    b, seq, _ = qkv3.shape
    hd = 2 * DA_HEAD_DIM
    tb = n_sub * tq
    kern = functools.partial(_diff_attn_kernel, tq=tq, tk=tk, lambda_init=lambda_init)
    vec = lambda n: pl.BlockSpec((1, n), lambda bi, h, qi: (0, 0))
    blk = (2 * (2 * seq * hd * 2 + 2 * tb * hd * 2) + 2 * n_sub * (2 * tq * LANES * 4 + tq * hd * 4)
           + 6 * tq * tk * 4)
    return pl.pallas_call(
        kern,
        out_shape=jax.ShapeDtypeStruct((b, seq, DA_HEADS * hd), jnp.bfloat16),
        grid=(b, DA_HEADS, seq // tb),
        in_specs=[
            vec(DA_HEAD_DIM), vec(DA_HEAD_DIM), vec(DA_HEAD_DIM), vec(DA_HEAD_DIM), vec(hd),
            pl.BlockSpec((1, tb, hd), lambda bi, h, qi: (bi, qi, h)),
            pl.BlockSpec((1, seq, hd), lambda bi, h, qi: (bi, 0, DA_HEADS + h)),
            pl.BlockSpec((1, seq, hd), lambda bi, h, qi: (bi, 0, 2 * DA_HEADS + h)),
        ],
        out_specs=pl.BlockSpec((1, tb, hd), lambda bi, h, qi: (bi, qi, h)),
        scratch_shapes=[
            pltpu.VMEM((n_sub, 2, tq, LANES), jnp.float32),
            pltpu.VMEM((n_sub, 2, tq, LANES), jnp.float32),
            pltpu.VMEM((n_sub, 2, tq, hd), jnp.float32),
        ] + [pltpu.VMEM((tq, tk), jnp.float32)] * 4 + [pltpu.VMEM((tq, LANES), jnp.float32)] * 4,
        compiler_params=pltpu.CompilerParams(
            dimension_semantics=("parallel", "parallel", "arbitrary"),
            vmem_limit_bytes=_vmem_limit(blk)),
        name="diff_attn",
    )(lq1, lk1, lq2, lk2, subw, qkv3, qkv3, qkv3)


def _neigh_bias(rpb, rows):
    heads = rpb.shape[0]
    rpb = rpb.astype(jnp.float32) * LOG2E
    qc = np.arange(GRID_W)[:, None]
    kc = np.arange(GRID_W)[None, :]
    col_start = np.clip(qc - NA_KW // 2, 0, GRID_W - NA_KW)
    col_ok = (kc >= col_start) & (kc < col_start + NA_KW)
    dc = kc - qc + (NA_KW - 1)
    tcol = jnp.full((heads, 2 * NA_KH - 1, GRID_W, GRID_W), NEG_BIG, jnp.float32)
    for v in range(2 * NA_KW - 1):
        tcol = jnp.where((col_ok & (dc == v))[None, None], rpb[:, :, v][:, :, None, None], tcol)
    nblk = rows // NA_QROWS
    sel = np.zeros((3, NA_QROWS, NA_KROWS, 2 * NA_KH - 1), np.float32)
    for c, (rb, key_row0) in enumerate(((0, 0), (1, 0), (nblk - 1, (nblk - 3) * NA_QROWS))):
        for qr in range(NA_QROWS):
            r = rb * NA_QROWS + qr
            row_start = min(max(r - NA_KH // 2, 0), rows - NA_KH)
            for kr in range(NA_KROWS):
                krow = key_row0 + kr
                if row_start <= krow < row_start + NA_KH:
                    sel[c, qr, kr, krow - r + NA_KH - 1] = 1.0
    table = jnp.einsum('cqkr,hrxy->chqxky', sel, tcol, precision=lax.Precision.HIGHEST)
    row_ok = sel.sum(axis=-1) > 0
    table = jnp.where(row_ok[:, None, :, None, :, None], table, NEG_BIG)
    return table.reshape(3, heads, NA_QROWS * GRID_W, NA_KROWS * GRID_W)


def _neigh_attn_kernel(bias_ref, q_ref, k0_ref, k1_ref, k2_ref, v0_ref, v1_ref, v2_ref, o_ref):
    d = NA_HEAD_DIM
    nkb = q_ref.shape[1]
    heads = q_ref.shape[2] // d
    k_refs = (k0_ref, k1_ref, k2_ref)
    v_refs = (v0_ref, v1_ref, v2_ref)

    def scores(h, token):
        cols = slice(h * d, (h + 1) * d)
        q = q_ref[0, :, cols]
        parts = []
        for k_ref in k_refs:
            k = k_ref[0, :, cols]
            if token is not None:
                k = k + token
            parts.append(lax.dot_general(q, k, _NT, preferred_element_type=jnp.float32))
        s = jnp.concatenate(parts, axis=1) + bias_ref[0, h]
        return s, jnp.max(s, axis=1, keepdims=True)

    def consume(h, s, m):
        cols = slice(h * d, (h + 1) * d)
        p = jnp.exp2(s - m)
        l = jnp.sum(p, axis=1, keepdims=True)
        pb = p.astype(v0_ref.dtype)
        o = None
        for t, v_ref in enumerate(v_refs):
            pv = jnp.dot(pb[:, t * nkb:(t + 1) * nkb], v_ref[0, :, cols], preferred_element_type=jnp.float32)
            o = pv if o is None else o + pv
        o_ref[0, :, cols] = (o / l).astype(o_ref.dtype)
        return _zero_after(o, k0_ref.dtype)

    nxt = scores(0, None)
    token = None
    for h in range(heads):
        cur = nxt
        if h + 1 < heads:
            nxt = scores(h + 1, token)
        token = consume(h, *cur)


def _neigh_attn(qkv3, bias):
    b, seq, _ = qkv3.shape
    width = NA_HEADS * NA_HEAD_DIM
    nq = NA_QROWS * GRID_W
    nk = NA_KROWS * GRID_W
    nblk = seq // nq
    col0 = 3 * DA_HEADS * 2 * DA_HEAD_DIM // width
    start = lambda rb: jnp.clip(rb - 1, 0, nblk - 3)
    block_class = lambda rb: jnp.where(rb == 0, 0, jnp.where(rb == nblk - 1, 2, 1))
    kv_spec = lambda base, t: pl.BlockSpec((1, nq, width), lambda bi, rb: (bi, start(rb) + t, col0 + base))
    blk = 2 * (NA_HEADS * nq * nk * 4 + 8 * nq * width * 2) + 8 * nq * nk * 4
    return pl.pallas_call(
        _neigh_attn_kernel,
        out_shape=jax.ShapeDtypeStruct((b, seq, width), jnp.bfloat16),
        grid=(b, nblk),
        in_specs=[
            pl.BlockSpec((1, NA_HEADS, nq, nk), lambda bi, rb: (block_class(rb), 0, 0, 0)),
            pl.BlockSpec((1, nq, width), lambda bi, rb: (bi, rb, col0)),
            kv_spec(1, 0), kv_spec(1, 1), kv_spec(1, 2),
            kv_spec(2, 0), kv_spec(2, 1), kv_spec(2, 2),
        ],
        out_specs=pl.BlockSpec((1, nq, width), lambda bi, rb: (bi, rb, 0)),
        compiler_params=pltpu.CompilerParams(
            dimension_semantics=("parallel", "arbitrary"),
            vmem_limit_bytes=_vmem_limit(blk)),
        name="neigh_attn",
    )(bias, qkv3, qkv3, qkv3, qkv3, qkv3, qkv3, qkv3)


def _merge_kernel(x_ref, gpre_ref, gpost_ref, oa_ref, on_ref, wga_ref, wgb_ref, wa_ref, wb_ref, wo_ref,
                  o_ref, h_ref, y_ref):
    j = pl.program_id(1)

    @pl.when(j == 0)
    def _():
        h_ref[...] = _rms(x_ref[...], gpre_ref[...]).astype(h_ref.dtype)
        y_ref[...] = jnp.zeros(y_ref.shape, jnp.float32)

    h = h_ref[...]
    f32 = jnp.float32
    ga = 1.0 / (1.0 + jnp.exp(-jnp.dot(h, wga_ref[...], preferred_element_type=f32)))
    gb = 1.0 / (1.0 + jnp.exp(-jnp.dot(h, wgb_ref[...], preferred_element_type=f32)))
    a = jnp.dot(oa_ref[...], wa_ref[...], preferred_element_type=f32)
    b = jnp.dot(on_ref[...], wb_ref[...], preferred_element_type=f32)
    mixed = (ga * a + gb * b).astype(wo_ref.dtype)
    y_ref[...] += jnp.dot(mixed, wo_ref[...], preferred_element_type=f32)

    @pl.when(j == pl.num_programs(1) - 1)
    def _():
        o_ref[...] = x_ref[...] + _rms(y_ref[...], gpost_ref[...])


def _merge(x2, gpre, gpost, oa2, on2, w_in_bf16, wa, wb, wo, *, tm, tn):
    m, d = x2.shape
    ca = oa2.shape[1]
    cb = on2.shape[1]
    gate_col0 = (w_in_bf16.shape[1] - 2 * d) // tn
    nj = d // tn
    blk = (2 * (2 * tm * d * 4 + tm * (ca + cb) * 2 + (2 * d + ca + cb + d) * tn * 2)
           + tm * d * 2 + tm * d * 4 + 6 * tm * tn * 4)
    return pl.pallas_call(
        _merge_kernel,
        out_shape=jax.ShapeDtypeStruct((m, d), jnp.float32),
        grid=(m // tm, nj),
        in_specs=[
            pl.BlockSpec((tm, d), lambda i, j: (i, 0)),
            pl.BlockSpec((1, d), lambda i, j: (0, 0)),
            pl.BlockSpec((1, d), lambda i, j: (0, 0)),
            pl.BlockSpec((tm, ca), lambda i, j: (i, 0)),
            pl.BlockSpec((tm, cb), lambda i, j: (i, 0)),
            pl.BlockSpec((d, tn), lambda i, j: (0, gate_col0 + j)),
            pl.BlockSpec((d, tn), lambda i, j: (0, gate_col0 + nj + j)),
            pl.BlockSpec((ca, tn), lambda i, j: (0, j)),
            pl.BlockSpec((cb, tn), lambda i, j: (0, j)),
            pl.BlockSpec((tn, d), lambda i, j: (j, 0)),
        ],
        out_specs=pl.BlockSpec((tm, d), lambda i, j: (i, 0)),
        scratch_shapes=[pltpu.VMEM((tm, d), jnp.bfloat16), pltpu.VMEM((tm, d), jnp.float32)],
        compiler_params=pltpu.CompilerParams(
            dimension_semantics=("parallel", "arbitrary"),
            vmem_limit_bytes=_vmem_limit(blk)),
        name="merge",
    )(x2, gpre, gpost, oa2, on2, w_in_bf16, w_in_bf16, wa, wb, wo)


def _mlp_kernel(x_ref, gpre_ref, gpost_ref, wu_ref, wd_ref, o_ref, h_ref, acc_ref):
    j = pl.program_id(1)

    @pl.when(j == 0)
    def _():
        h_ref[...] = _rms(x_ref[...], gpre_ref[...]).astype(h_ref.dtype)
        acc_ref[...] = jnp.zeros(acc_ref.shape, jnp.float32)

    u = jnp.dot(h_ref[...], wu_ref[...], preferred_element_type=jnp.float32)
    u = jnp.square(jnp.maximum(u, 0.0)).astype(wd_ref.dtype)
    acc_ref[...] += jnp.dot(u, wd_ref[...], preferred_element_type=jnp.float32)

    @pl.when(j == pl.num_programs(1) - 1)
    def _():
        o_ref[...] = x_ref[...] + _rms(acc_ref[...], gpost_ref[...])


def _mlp(x2, gpre, gpost, wu, wd, *, tm, tf):
    m, d = x2.shape
    f = wu.shape[1]
    blk = 2 * (2 * tm * d * 4 + 2 * d * tf * 2) + tm * d * 2 + tm * d * 4 + 2 * tm * tf * 4
    return pl.pallas_call(
        _mlp_kernel,
        out_shape=jax.ShapeDtypeStruct((m, d), jnp.float32),
        grid=(m // tm, f // tf),
        in_specs=[
            pl.BlockSpec((tm, d), lambda i, j: (i, 0)),
            pl.BlockSpec((1, d), lambda i, j: (0, 0)),
            pl.BlockSpec((1, d), lambda i, j: (0, 0)),
            pl.BlockSpec((d, tf), lambda i, j: (0, j)),
            pl.BlockSpec((tf, d), lambda i, j: (j, 0)),
        ],
        out_specs=pl.BlockSpec((tm, d), lambda i, j: (i, 0)),
        scratch_shapes=[pltpu.VMEM((tm, d), jnp.bfloat16), pltpu.VMEM((tm, d), jnp.float32)],
        compiler_params=pltpu.CompilerParams(
            dimension_semantics=("parallel", "arbitrary"),
            vmem_limit_bytes=_vmem_limit(blk)),
        name="mlp",
    )(x2, gpre, gpost, wu, wd)


def _rope_tables(seq):
    inv = (1.0 / (np.float32(ROPE_THETA) ** (np.arange(0, DA_HEAD_DIM, 2, dtype=np.float32) / DA_HEAD_DIM)))
    ang = np.arange(seq, dtype=np.float32)[:, None] * inv.astype(np.float32)[None, :]
    cos, sin = np.cos(ang).astype(np.float32), np.sin(ang).astype(np.float32)
    return jnp.asarray(np.concatenate([cos, cos], axis=1)), jnp.asarray(np.concatenate([-sin, sin], axis=1))


def kernel(x, w_in, w_branch_a, w_branch_b, w_out, norm_mix_pre, norm_mix_post, norm_mlp_pre, norm_mlp_post,
           lam_q1, lam_k1, lam_q2, lam_k2, subln_w, na_rpb, w_up, w_down):
    b, seq, d = x.shape
    m = b * seq
    depth = w_in.shape[0]
    da_width = DA_HEADS * 2 * DA_HEAD_DIM
    na_width = NA_HEADS * NA_HEAD_DIM
    qkv_cols = 3 * da_width + 3 * na_width
    bf16 = jnp.bfloat16
    cosd, sind = _rope_tables(seq)
    row = lambda v: v.reshape(1, -1).astype(jnp.float32)
    tm = min(512, seq)
    tq = min(512, seq)
    x2 = x.reshape(m, d)
    for l in range(depth):
        lambda_init = 0.8 - 0.6 * math.exp(-0.3 * l)
        w_in_l = w_in[l].astype(bf16)
        qkv = _qkv_proj(x2, row(norm_mix_pre[l]), w_in_l, cosd, sind, seq,
                        tm=min(1024, seq), tn=2048, da_width=da_width, na_width=na_width)
        qkv3 = qkv.reshape(b, seq, qkv_cols)
        oa = _diff_attn(qkv3, row(lam_q1[l]), row(lam_k1[l]), row(lam_q2[l]), row(lam_k2[l]), row(subln_w[l]),
                        tq=tq, tk=min(1024, seq), n_sub=min(2, seq // tq), lambda_init=lambda_init)
        on = _neigh_attn(qkv3, _neigh_bias(na_rpb[l], seq // GRID_W))
        x2 = _merge(x2, row(norm_mix_pre[l]), row(norm_mix_post[l]), oa.reshape(m, da_width),
                    on.reshape(m, na_width), w_in_l, w_branch_a[l].astype(bf16), w_branch_b[l].astype(bf16),
                    w_out[l].astype(bf16), tm=tm, tn=512)
        x2 = _mlp(x2, row(norm_mlp_pre[l]), row(norm_mlp_post[l]), w_up[l].astype(bf16), w_down[l].astype(bf16),
                  tm=tm, tf=min(1024, w_up.shape[2]))
    return x2.reshape(b, seq, d)
```

```python
import functools
import math

import jax
import jax.numpy as jnp
import numpy as np
from jax import lax
from jax.experimental import pallas as pl
from jax.experimental.pallas import tpu as pltpu

EPS = 1e-6
ROPE_THETA = 10000.0
GRID_W = 64
DA_HEADS = 4
DA_HEAD_DIM = 128
NA_HEADS = 8
NA_HEAD_DIM = 128
NA_KH = 8
NA_KW = 16
LANES = 128
SUBLANES = 8
MXU_TILE = 256
NEG_BIG = -1e30
NA_QROWS = 4
NA_KROWS = 12
V7X_VMEM_BYTES = 64 * 1024 * 1024
assert DA_HEAD_DIM == NA_HEAD_DIM
LOG2E = math.log2(math.e)
QK_MULT = DA_HEAD_DIM ** -0.5 * LOG2E

_NT = (((1,), (1,)), ((), ()))


def _vmem_limit(block_bytes):
    return int(min(block_bytes + 16 * 1024 * 1024, V7X_VMEM_BYTES - 6 * 1024 * 1024))


def _rms(x, g):
    return x * lax.rsqrt(jnp.mean(x * x, axis=-1, keepdims=True) + EPS) * g


def _zero_after(x, dtype):
    bits = pltpu.bitcast(x[0:SUBLANES, 0:LANES], jnp.uint32)
    return pltpu.bitcast((bits >> 16) >> 16, jnp.float32)[0:1, :].astype(dtype)


def _qkv_kernel(x_ref, g_ref, w_ref, cos_ref, sin_ref, o_ref, h_ref, *, group_cols):
    j = pl.program_id(1)

    @pl.when(j == 0)
    def _():
        h_ref[...] = _rms(x_ref[...], g_ref[...]).astype(h_ref.dtype)

    acc = jnp.dot(h_ref[...], w_ref[...], preferred_element_type=jnp.float32)

    groups_per_tile = acc.shape[1] // group_cols
    for gi in range(groups_per_tile):
        group = j * groups_per_tile + gi
        rot = jnp.where(group == 0, QK_MULT, jnp.where(group == 1, 1.0, 0.0)).astype(jnp.float32)
        flat = jnp.where(group == 3, QK_MULT, jnp.where(group <= 1, 0.0, 1.0)).astype(jnp.float32)
        cs = cos_ref[...] * rot + flat
        sn = sin_ref[...] * rot
        for g in range(gi * group_cols // LANES, (gi + 1) * group_cols // LANES):
            y = acc[:, g * LANES:(g + 1) * LANES]
            o_ref[:, g * LANES:(g + 1) * LANES] = (y * cs + pltpu.roll(y, LANES // 2, 1) * sn).astype(o_ref.dtype)


def _qkv_proj(x2, gain, w_bf16, cosd, sind, seq, *, tm, tn, da_width, na_width):
    m, d = x2.shape
    assert da_width == na_width and tn % da_width == 0
    n_cols = 3 * da_width + 3 * na_width
    pos_blocks = seq // tm
    blk = 2 * (tm * d * 4 + d * tn * 2 + tm * tn * 2 + 2 * tm * LANES * 4 + tm * d * 2) + tm * tn * 4
    return pl.pallas_call(
        functools.partial(_qkv_kernel, group_cols=da_width),
        out_shape=(jax.ShapeDtypeStruct((m, n_cols), jnp.bfloat16), jax.ShapeDtypeStruct((m, d), jnp.bfloat16)),
        grid=(m // tm, n_cols // tn),
        in_specs=[
            pl.BlockSpec((tm, d), lambda i, j: (i, 0)),
            pl.BlockSpec((1, d), lambda i, j: (0, 0)),
            pl.BlockSpec((d, tn), lambda i, j: (0, j)),
            pl.BlockSpec((tm, LANES), lambda i, j: (i % pos_blocks, 0)),
            pl.BlockSpec((tm, LANES), lambda i, j: (i % pos_blocks, 0)),
        ],
        out_specs=(pl.BlockSpec((tm, tn), lambda i, j: (i, j)), pl.BlockSpec((tm, d), lambda i, j: (i, 0))),
        compiler_params=pltpu.CompilerParams(
            dimension_semantics=("parallel", "arbitrary"),
            vmem_limit_bytes=_vmem_limit(blk)),
        name="qkv_proj",
    )(x2, gain, w_bf16, cosd, sind)


def _diff_attn_kernel(lq1_ref, lk1_ref, lq2_ref, lk2_ref, subw_ref, q_ref, k_ref, v_ref, o_ref,
                      m_ref, l_ref, a_ref, sa1, sa2, sb1, sb2, ma1, ma2, mb1, mb2, *, tq, tk, lambda_init):
    seq = k_ref.shape[1]
    n_chunks = seq // tk
    n_sub = q_ref.shape[1] // tq
    d = DA_HEAD_DIM
    lam = (jnp.exp(jnp.sum(lq1_ref[...] * lk1_ref[...])) - jnp.exp(jnp.sum(lq2_ref[...] * lk2_ref[...]))
           + lambda_init)

    def scores(item, dst, token):
        sub, c = item
        for half in range(2):
            q = q_ref[0, sub * tq:(sub + 1) * tq, half * d:(half + 1) * d]
            mx = None
            for t in range(tk // MXU_TILE):
                k = k_ref[0, pl.ds(c * tk + t * MXU_TILE, MXU_TILE), half * d:(half + 1) * d]
                if token is not None:
                    k = k + token
                s = lax.dot_general(q, k, _NT, preferred_element_type=jnp.float32)
                dst[half][:, t * MXU_TILE:(t + 1) * MXU_TILE] = s
                for u in range(MXU_TILE // LANES):
                    su = s[:, u * LANES:(u + 1) * LANES]
                    mx = su if mx is None else jnp.maximum(mx, su)
            dst[2 + half][...] = jnp.broadcast_to(jnp.max(mx, axis=1, keepdims=True), mx.shape)

    def consume(item, src):
        sub, c = item
        token = None
        for half in range(2):
            if c == 0:
                m_prev = jnp.full((tq, LANES), NEG_BIG, jnp.float32)
            else:
                m_prev = m_ref[sub, half]
            m_new = jnp.maximum(m_prev, src[2 + half][...])
            alpha = jnp.exp2(m_prev - m_new)
            m_rep = jnp.concatenate([m_new] * (MXU_TILE // LANES), axis=1)
            psum = None
            pv = None
            for t in range(tk // MXU_TILE):
                p = jnp.exp2(src[half][:, t * MXU_TILE:(t + 1) * MXU_TILE] - m_rep)
                for u in range(MXU_TILE // LANES):
                    pu = p[:, u * LANES:(u + 1) * LANES]
                    psum = pu if psum is None else psum + pu
                v = v_ref[0, pl.ds(c * tk + t * MXU_TILE, MXU_TILE), :]
                pvt = jnp.dot(p.astype(v.dtype), v, preferred_element_type=jnp.float32)
                pv = pvt if pv is None else pv + pvt
            if c == 0:
                l_ref[sub, half] = psum
                a_ref[sub, half] = pv
            else:
                l_ref[sub, half] = alpha * l_ref[sub, half] + psum
                a_ref[sub, half] = a_ref[sub, half] * jnp.concatenate([alpha] * (2 * d // LANES), axis=1) + pv
            m_ref[sub, half] = m_new
            if half == 0:
                token = _zero_after(pv, k_ref.dtype)
        return token

    def finalize(sub):
        o1 = a_ref[sub, 0] / jnp.sum(l_ref[sub, 0], axis=1, keepdims=True)
        o2 = a_ref[sub, 1] / jnp.sum(l_ref[sub, 1], axis=1, keepdims=True)
        o = _rms(o1 - lam * o2, subw_ref[...]) * (1.0 - lambda_init)
        o_ref[0, sub * tq:(sub + 1) * tq, :] = o.astype(o_ref.dtype)

    bufs = ((sa1, sa2, ma1, ma2), (sb1, sb2, mb1, mb2))
    items = [(sub, c) for sub in range(n_sub) for c in range(n_chunks)]
    scores(items[0], bufs[0], None)
    token = None
    for n, item in enumerate(items):
        if n + 1 < len(items):
            scores(items[n + 1], bufs[(n + 1) % 2], token)
        token = consume(item, bufs[n % 2])
        if item[1] == n_chunks - 1:
            finalize(item[0])


def _diff_attn(qkv3, lq1, lk1, lq2, lk2, subw, *, tq, tk, n_sub, lambda_init):
    b, seq, _ = qkv3.shape
    hd = 2 * DA_HEAD_DIM
    tb = n_sub * tq
    kern = functools.partial(_diff_attn_kernel, tq=tq, tk=tk, lambda_init=lambda_init)
    vec = lambda n: pl.BlockSpec((1, n), lambda bi, h, qi: (0, 0))
    blk = (2 * (2 * seq * hd * 2 + 2 * tb * hd * 2) + 2 * n_sub * (2 * tq * LANES * 4 + tq * hd * 4)
           + 6 * tq * tk * 4)
    return pl.pallas_call(
        kern,
        out_shape=jax.ShapeDtypeStruct((b, seq, DA_HEADS * hd), jnp.bfloat16),
        grid=(b, DA_HEADS, seq // tb),
        in_specs=[
            vec(DA_HEAD_DIM), vec(DA_HEAD_DIM), vec(DA_HEAD_DIM), vec(DA_HEAD_DIM), vec(hd),
            pl.BlockSpec((1, tb, hd), lambda bi, h, qi: (bi, qi, h)),
            pl.BlockSpec((1, seq, hd), lambda bi, h, qi: (bi, 0, DA_HEADS + h)),
            pl.BlockSpec((1, seq, hd), lambda bi, h, qi: (bi, 0, 2 * DA_HEADS + h)),
        ],
        out_specs=pl.BlockSpec((1, tb, hd), lambda bi, h, qi: (bi, qi, h)),
        scratch_shapes=[
            pltpu.VMEM((n_sub, 2, tq, LANES), jnp.float32),
            pltpu.VMEM((n_sub, 2, tq, LANES), jnp.float32),
            pltpu.VMEM((n_sub, 2, tq, hd), jnp.float32),
        ] + [pltpu.VMEM((tq, tk), jnp.float32)] * 4 + [pltpu.VMEM((tq, LANES), jnp.float32)] * 4,
        compiler_params=pltpu.CompilerParams(
            dimension_semantics=("parallel", "parallel", "arbitrary"),
            vmem_limit_bytes=_vmem_limit(blk)),
        name="diff_attn",
    )(lq1, lk1, lq2, lk2, subw, qkv3, qkv3, qkv3)


def _neigh_bias(rpb, rows):
    heads = rpb.shape[0]
    rpb = rpb.astype(jnp.float32) * LOG2E
    qc = np.arange(GRID_W)[:, None]
    kc = np.arange(GRID_W)[None, :]
    col_start = np.clip(qc - NA_KW // 2, 0, GRID_W - NA_KW)
    col_ok = (kc >= col_start) & (kc < col_start + NA_KW)
    dc = kc - qc + (NA_KW - 1)
    tcol = jnp.full((heads, 2 * NA_KH - 1, GRID_W, GRID_W), NEG_BIG, jnp.float32)
    for v in range(2 * NA_KW - 1):
        tcol = jnp.where((col_ok & (dc == v))[None, None], rpb[:, :, v][:, :, None, None], tcol)
    strip = jnp.transpose(tcol, (0, 2, 1, 3)).reshape(heads, GRID_W, (2 * NA_KH - 1) * GRID_W)
    masked = lambda n: jnp.full((heads, GRID_W, n * GRID_W), NEG_BIG, jnp.float32)
    nblk = rows // NA_QROWS
    tables = []
    for rb, key_row0 in ((0, 0), (1, 0), (nblk - 1, (nblk - 3) * NA_QROWS)):
        lines = []
        for qr in range(NA_QROWS):
            r = rb * NA_QROWS + qr
            row_start = min(max(r - NA_KH // 2, 0), rows - NA_KH)
            kr_lo = max(row_start - key_row0, 0)
            kr_hi = min(row_start + NA_KH - 1 - key_row0, NA_KROWS - 1)
            dr_lo = key_row0 + kr_lo - r + NA_KH - 1
            pieces = [strip[:, :, dr_lo * GRID_W:(dr_lo + kr_hi - kr_lo + 1) * GRID_W]]
            if kr_lo > 0:
                pieces.insert(0, masked(kr_lo))
            if kr_hi < NA_KROWS - 1:
                pieces.append(masked(NA_KROWS - 1 - kr_hi))
            lines.append(jnp.concatenate(pieces, axis=-1))
        tables.append(jnp.concatenate(lines, axis=1))
    return jnp.stack(tables)


def _neigh_attn_kernel(bias_ref, q_ref, k0_ref, k1_ref, k2_ref, v0_ref, v1_ref, v2_ref, o_ref):
    d = NA_HEAD_DIM
    nkb = q_ref.shape[1]
    heads = q_ref.shape[2] // d
    k_refs = (k0_ref, k1_ref, k2_ref)
    v_refs = (v0_ref, v1_ref, v2_ref)

    def scores(h, token):
        cols = slice(h * d, (h + 1) * d)
        q = q_ref[0, :, cols]
        parts = []
        for k_ref in k_refs:
            k = k_ref[0, :, cols]
            if token is not None:
                k = k + token
            parts.append(lax.dot_general(q, k, _NT, preferred_element_type=jnp.float32))
        s = jnp.concatenate(parts, axis=1) + bias_ref[0, h]
        return s, jnp.max(s, axis=1, keepdims=True)

    def consume(h, s, m):
        cols = slice(h * d, (h + 1) * d)
        p = jnp.exp2(s - m)
        l = jnp.sum(p, axis=1, keepdims=True)
        pb = p.astype(v0_ref.dtype)
        o = None
        for t, v_ref in enumerate(v_refs):
            pv = jnp.dot(pb[:, t * nkb:(t + 1) * nkb], v_ref[0, :, cols], preferred_element_type=jnp.float32)
            o = pv if o is None else o + pv
        o_ref[0, :, cols] = (o / l).astype(o_ref.dtype)
        return _zero_after(o, k0_ref.dtype)

    nxt = scores(0, None)
    token = None
    for h in range(heads):
        cur = nxt
        if h + 1 < heads:
            nxt = scores(h + 1, token)
        token = consume(h, *cur)


def _neigh_attn(qkv3, bias):
    b, seq, _ = qkv3.shape
    width = NA_HEADS * NA_HEAD_DIM
    nq = NA_QROWS * GRID_W
    nk = NA_KROWS * GRID_W
    nblk = seq // nq
    col0 = 3 * DA_HEADS * 2 * DA_HEAD_DIM // width
    start = lambda rb: jnp.clip(rb - 1, 0, nblk - 3)
    block_class = lambda rb: jnp.where(rb == 0, 0, jnp.where(rb == nblk - 1, 2, 1))
    kv_spec = lambda base, t: pl.BlockSpec((1, nq, width), lambda bi, rb: (bi, start(rb) + t, col0 + base))
    blk = 2 * (NA_HEADS * nq * nk * 4 + 8 * nq * width * 2) + 8 * nq * nk * 4
    return pl.pallas_call(
        _neigh_attn_kernel,
        out_shape=jax.ShapeDtypeStruct((b, seq, width), jnp.bfloat16),
        grid=(b, nblk),
        in_specs=[
            pl.BlockSpec((1, NA_HEADS, nq, nk), lambda bi, rb: (block_class(rb), 0, 0, 0)),
            pl.BlockSpec((1, nq, width), lambda bi, rb: (bi, rb, col0)),
            kv_spec(1, 0), kv_spec(1, 1), kv_spec(1, 2),
            kv_spec(2, 0), kv_spec(2, 1), kv_spec(2, 2),
        ],
        out_specs=pl.BlockSpec((1, nq, width), lambda bi, rb: (bi, rb, 0)),
        compiler_params=pltpu.CompilerParams(
            dimension_semantics=("parallel", "arbitrary"),
            vmem_limit_bytes=_vmem_limit(blk)),
        name="neigh_attn",
    )(bias, qkv3, qkv3, qkv3, qkv3, qkv3, qkv3, qkv3)


def _merge_kernel(x_ref, h_ref, gpost_ref, gnext_ref, oa_ref, on_ref, wga_ref, wgb_ref, wa_ref, wb_ref, wo_ref,
                  o_ref, hn_ref, y_ref):
    j = pl.program_id(1)

    def accumulate(first):
        h = h_ref[...]
        f32 = jnp.float32
        ga = 1.0 / (1.0 + jnp.exp(-jnp.dot(h, wga_ref[...], preferred_element_type=f32)))
        gb = 1.0 / (1.0 + jnp.exp(-jnp.dot(h, wgb_ref[...], preferred_element_type=f32)))
        a = jnp.dot(oa_ref[...], wa_ref[...], preferred_element_type=f32)
        b = jnp.dot(on_ref[...], wb_ref[...], preferred_element_type=f32)
        mixed = (ga * a + gb * b).astype(wo_ref.dtype)
        part = jnp.dot(mixed, wo_ref[...], preferred_element_type=f32)
        y_ref[...] = part if first else y_ref[...] + part

    pl.when(j == 0)(lambda: accumulate(True))
    pl.when(j > 0)(lambda: accumulate(False))

    @pl.when(j == pl.num_programs(1) - 1)
    def _():
        x1 = x_ref[...] + _rms(y_ref[...], gpost_ref[...])
        o_ref[...] = x1
        hn_ref[...] = _rms(x1, gnext_ref[...]).astype(hn_ref.dtype)


def _merge(x2, h2, gpost, gnext, oa2, on2, w_in_bf16, wa, wb, wo, *, tm, tn):
    m, d = x2.shape
    ca = oa2.shape[1]
    cb = on2.shape[1]
    gate_col0 = (w_in_bf16.shape[1] - 2 * d) // tn
    nj = d // tn
    blk = (2 * (2 * tm * d * 4 + 2 * tm * d * 2 + tm * (ca + cb) * 2 + (2 * d + ca + cb + d) * tn * 2)
           + tm * d * 4 + 6 * tm * tn * 4)
    return pl.pallas_call(
        _merge_kernel,
        out_shape=(jax.ShapeDtypeStruct((m, d), jnp.float32), jax.ShapeDtypeStruct((m, d), jnp.bfloat16)),
        grid=(m // tm, nj),
        in_specs=[
            pl.BlockSpec((tm, d), lambda i, j: (i, 0)),
            pl.BlockSpec((tm, d), lambda i, j: (i, 0)),
            pl.BlockSpec((1, d), lambda i, j: (0, 0)),
            pl.BlockSpec((1, d), lambda i, j: (0, 0)),
            pl.BlockSpec((tm, ca), lambda i, j: (i, 0)),
            pl.BlockSpec((tm, cb), lambda i, j: (i, 0)),
            pl.BlockSpec((d, tn), lambda i, j: (0, gate_col0 + j)),
            pl.BlockSpec((d, tn), lambda i, j: (0, gate_col0 + nj + j)),
            pl.BlockSpec((ca, tn), lambda i, j: (0, j)),
            pl.BlockSpec((cb, tn), lambda i, j: (0, j)),
            pl.BlockSpec((tn, d), lambda i, j: (j, 0)),
        ],
        out_specs=(pl.BlockSpec((tm, d), lambda i, j: (i, 0)), pl.BlockSpec((tm, d), lambda i, j: (i, 0))),
        scratch_shapes=[pltpu.VMEM((tm, d), jnp.float32)],
        compiler_params=pltpu.CompilerParams(
            dimension_semantics=("parallel", "arbitrary"),
            vmem_limit_bytes=_vmem_limit(blk)),
        name="merge",
    )(x2, h2, gpost, gnext, oa2, on2, w_in_bf16, w_in_bf16, wa, wb, wo)


def _mlp_kernel(x_ref, h_ref, gpost_ref, wu_ref, wd_ref, o_ref, acc_ref):
    j = pl.program_id(1)

    def accumulate(first):
        u = jnp.dot(h_ref[...], wu_ref[...], preferred_element_type=jnp.float32)
        u = jnp.square(jnp.maximum(u, 0.0)).astype(wd_ref.dtype)
        part = jnp.dot(u, wd_ref[...], preferred_element_type=jnp.float32)
        acc_ref[...] = part if first else acc_ref[...] + part

    pl.when(j == 0)(lambda: accumulate(True))
    pl.when(j > 0)(lambda: accumulate(False))

    @pl.when(j == pl.num_programs(1) - 1)
    def _():
        o_ref[...] = x_ref[...] + _rms(acc_ref[...], gpost_ref[...])


def _mlp(x2, h2, gpost, wu, wd, *, tm, tf):
    m, d = x2.shape
    f = wu.shape[1]
    blk = 2 * (2 * tm * d * 4 + tm * d * 2 + 2 * d * tf * 2) + tm * d * 4 + 2 * tm * tf * 4
    return pl.pallas_call(
        _mlp_kernel,
        out_shape=jax.ShapeDtypeStruct((m, d), jnp.float32),
        grid=(m // tm, f // tf),
        in_specs=[
            pl.BlockSpec((tm, d), lambda i, j: (i, 0)),
            pl.BlockSpec((tm, d), lambda i, j: (i, 0)),
            pl.BlockSpec((1, d), lambda i, j: (0, 0)),
            pl.BlockSpec((d, tf), lambda i, j: (0, j)),
            pl.BlockSpec((tf, d), lambda i, j: (j, 0)),
        ],
        out_specs=pl.BlockSpec((tm, d), lambda i, j: (i, 0)),
        scratch_shapes=[pltpu.VMEM((tm, d), jnp.float32)],
        compiler_params=pltpu.CompilerParams(
            dimension_semantics=("parallel", "arbitrary"),
            vmem_limit_bytes=_vmem_limit(blk)),
        name="mlp",
    )(x2, h2, gpost, wu, wd)


def _rope_tables(seq):
    inv = (1.0 / (np.float32(ROPE_THETA) ** (np.arange(0, DA_HEAD_DIM, 2, dtype=np.float32) / DA_HEAD_DIM)))
    ang = np.arange(seq, dtype=np.float32)[:, None] * inv.astype(np.float32)[None, :]
    cos, sin = np.cos(ang).astype(np.float32), np.sin(ang).astype(np.float32)
    return jnp.asarray(np.concatenate([cos, cos], axis=1)), jnp.asarray(np.concatenate([-sin, sin], axis=1))


def kernel(x, w_in, w_branch_a, w_branch_b, w_out, norm_mix_pre, norm_mix_post, norm_mlp_pre, norm_mlp_post,
           lam_q1, lam_k1, lam_q2, lam_k2, subln_w, na_rpb, w_up, w_down):
    b, seq, d = x.shape
    m = b * seq
    depth = w_in.shape[0]
    da_width = DA_HEADS * 2 * DA_HEAD_DIM
    na_width = NA_HEADS * NA_HEAD_DIM
    qkv_cols = 3 * da_width + 3 * na_width
    bf16 = jnp.bfloat16
    cosd, sind = _rope_tables(seq)
    row = lambda v: v.reshape(1, -1).astype(jnp.float32)
    tm = min(512, seq)
    tq = min(512, seq)
    x2 = x.reshape(m, d)
    for l in range(depth):
        lambda_init = 0.8 - 0.6 * math.exp(-0.3 * l)
        w_in_l = w_in[l].astype(bf16)
        qkv, h_mix = _qkv_proj(x2, row(norm_mix_pre[l]), w_in_l, cosd, sind, seq,
                               tm=min(1024, seq), tn=2048, da_width=da_width, na_width=na_width)
        qkv3 = qkv.reshape(b, seq, qkv_cols)
        oa = _diff_attn(qkv3, row(lam_q1[l]), row(lam_k1[l]), row(lam_q2[l]), row(lam_k2[l]), row(subln_w[l]),
                        tq=tq, tk=min(1024, seq), n_sub=min(2, seq // tq), lambda_init=lambda_init)
        on = _neigh_attn(qkv3, _neigh_bias(na_rpb[l], seq // GRID_W))
        x2, h_mlp = _merge(x2, h_mix, row(norm_mix_post[l]), row(norm_mlp_pre[l]), oa.reshape(m, da_width),
                           on.reshape(m, na_width), w_in_l, w_branch_a[l].astype(bf16), w_branch_b[l].astype(bf16),
                           w_out[l].astype(bf16), tm=tm, tn=512)
        x2 = _mlp(x2, h_mlp, row(norm_mlp_post[l]), w_up[l].astype(bf16), w_down[l].astype(bf16),
                  tm=tm, tf=min(1024, w_up.shape[2]))
    return x2.reshape(b, seq, d)
```

```python
import functools
import math

import jax
import jax.numpy as jnp
import numpy as np
from jax import lax
from jax.experimental import pallas as pl
from jax.experimental.pallas import tpu as pltpu

EPS = 1e-6
ROPE_THETA = 10000.0
GRID_W = 64
DA_HEADS = 4
DA_HEAD_DIM = 128
NA_HEADS = 8
NA_HEAD_DIM = 128
NA_KH = 8
NA_KW = 16
LANES = 128
SUBLANES = 8
MXU_TILE = 256
NEG_BIG = -1e30
NA_QROWS = 4
NA_KROWS = 12
V7X_VMEM_BYTES = 64 * 1024 * 1024
assert DA_HEAD_DIM == NA_HEAD_DIM
LOG2E = math.log2(math.e)
QK_MULT = DA_HEAD_DIM ** -0.5 * LOG2E

_NT = (((1,), (1,)), ((), ()))


def _vmem_limit(block_bytes):
    return int(min(block_bytes + 16 * 1024 * 1024, V7X_VMEM_BYTES - 6 * 1024 * 1024))


def _rms(x, g):
    return x * lax.rsqrt(jnp.mean(x * x, axis=-1, keepdims=True) + EPS) * g


def _zero_after(x, dtype):
    bits = pltpu.bitcast(x[0:SUBLANES, 0:LANES], jnp.uint32)
    return pltpu.bitcast((bits >> 16) >> 16, jnp.float32)[0:1, :].astype(dtype)


def _qkv_kernel(x_ref, g_ref, w_ref, cos_ref, sin_ref, o_ref, h_ref, *, group_cols):
    j = pl.program_id(1)

    @pl.when(j == 0)
    def _():
        h_ref[...] = _rms(x_ref[...], g_ref[...]).astype(h_ref.dtype)

    acc = jnp.dot(h_ref[...], w_ref[...], preferred_element_type=jnp.float32)

    groups_per_tile = acc.shape[1] // group_cols
    for gi in range(groups_per_tile):
        group = j * groups_per_tile + gi
        rot = jnp.where(group == 0, QK_MULT, jnp.where(group == 1, 1.0, 0.0)).astype(jnp.float32)
        flat = jnp.where(group == 3, QK_MULT, jnp.where(group <= 1, 0.0, 1.0)).astype(jnp.float32)
        cs = cos_ref[...] * rot + flat
        sn = sin_ref[...] * rot
        for g in range(gi * group_cols // LANES, (gi + 1) * group_cols // LANES):
            y = acc[:, g * LANES:(g + 1) * LANES]
            o_ref[:, g * LANES:(g + 1) * LANES] = (y * cs + pltpu.roll(y, LANES // 2, 1) * sn).astype(o_ref.dtype)


def _qkv_proj(x2, gain, w_bf16, cosd, sind, seq, *, tm, tn, da_width, na_width):
    m, d = x2.shape
    assert da_width == na_width and tn % da_width == 0
    n_cols = 3 * da_width + 3 * na_width
    pos_blocks = seq // tm
    blk = 2 * (tm * d * 4 + d * tn * 2 + tm * tn * 2 + 2 * tm * LANES * 4 + tm * d * 2) + tm * tn * 4
    return pl.pallas_call(
        functools.partial(_qkv_kernel, group_cols=da_width),
        out_shape=(jax.ShapeDtypeStruct((m, n_cols), jnp.bfloat16), jax.ShapeDtypeStruct((m, d), jnp.bfloat16)),
        grid=(m // tm, n_cols // tn),
        in_specs=[
            pl.BlockSpec((tm, d), lambda i, j: (i, 0)),
            pl.BlockSpec((1, d), lambda i, j: (0, 0)),
            pl.BlockSpec((d, tn), lambda i, j: (0, j)),
            pl.BlockSpec((tm, LANES), lambda i, j: (i % pos_blocks, 0)),
            pl.BlockSpec((tm, LANES), lambda i, j: (i % pos_blocks, 0)),
        ],
        out_specs=(pl.BlockSpec((tm, tn), lambda i, j: (i, j)), pl.BlockSpec((tm, d), lambda i, j: (i, 0))),
        compiler_params=pltpu.CompilerParams(
            dimension_semantics=("parallel", "arbitrary"),
            vmem_limit_bytes=_vmem_limit(blk)),
        name="qkv_proj",
    )(x2, gain, w_bf16, cosd, sind)


def _diff_attn_kernel(lq1_ref, lk1_ref, lq2_ref, lk2_ref, subw_ref, q_ref, k_ref, v_ref, o_ref,
                      m_ref, l_ref, a_ref, sa1, sa2, sb1, sb2, ma1, ma2, mb1, mb2, *, tq, tk, lambda_init):
    seq = k_ref.shape[1]
    n_chunks = seq // tk
    n_sub = q_ref.shape[1] // tq
    d = DA_HEAD_DIM
    lam = (jnp.exp(jnp.sum(lq1_ref[...] * lk1_ref[...])) - jnp.exp(jnp.sum(lq2_ref[...] * lk2_ref[...]))
           + lambda_init)

    def scores(item, dst, token):
        sub, c = item
        for half in range(2):
            q = q_ref[0, sub * tq:(sub + 1) * tq, half * d:(half + 1) * d]
            mx = None
            for t in range(tk // MXU_TILE):
                k = k_ref[0, pl.ds(c * tk + t * MXU_TILE, MXU_TILE), half * d:(half + 1) * d]
                if token is not None:
                    k = k + token
                s = lax.dot_general(q, k, _NT, preferred_element_type=jnp.float32)
                dst[half][:, t * MXU_TILE:(t + 1) * MXU_TILE] = s
                for u in range(MXU_TILE // LANES):
                    su = s[:, u * LANES:(u + 1) * LANES]
                    mx = su if mx is None else jnp.maximum(mx, su)
            dst[2 + half][...] = jnp.broadcast_to(jnp.max(mx, axis=1, keepdims=True), mx.shape)

    def consume(item, src):
        sub, c = item
        token = None
        for half in range(2):
            if c == 0:
                m_prev = jnp.full((tq, LANES), NEG_BIG, jnp.float32)
            else:
                m_prev = m_ref[sub, half]
            m_new = jnp.maximum(m_prev, src[2 + half][...])
            alpha = jnp.exp2(m_prev - m_new)
            m_rep = jnp.concatenate([m_new] * (MXU_TILE // LANES), axis=1)
            psum = None
            pv = None
            for t in range(tk // MXU_TILE):
                p = jnp.exp2(src[half][:, t * MXU_TILE:(t + 1) * MXU_TILE] - m_rep)
                for u in range(MXU_TILE // LANES):
                    pu = p[:, u * LANES:(u + 1) * LANES]
                    psum = pu if psum is None else psum + pu
                v = v_ref[0, pl.ds(c * tk + t * MXU_TILE, MXU_TILE), :]
                pvt = jnp.dot(p.astype(v.dtype), v, preferred_element_type=jnp.float32)
                pv = pvt if pv is None else pv + pvt
            if c == 0:
                l_ref[sub, half] = psum
                a_ref[sub, half] = pv
            else:
                l_ref[sub, half] = alpha * l_ref[sub, half] + psum
                a_ref[sub, half] = a_ref[sub, half] * jnp.concatenate([alpha] * (2 * d // LANES), axis=1) + pv
            m_ref[sub, half] = m_new
            if half == 0:
                token = _zero_after(pv, k_ref.dtype)
        return token

    def finalize(sub):
        o1 = a_ref[sub, 0] / jnp.sum(l_ref[sub, 0], axis=1, keepdims=True)
        o2 = a_ref[sub, 1] / jnp.sum(l_ref[sub, 1], axis=1, keepdims=True)
        o = _rms(o1 - lam * o2, subw_ref[...]) * (1.0 - lambda_init)
        o_ref[0, sub * tq:(sub + 1) * tq, :] = o.astype(o_ref.dtype)

    bufs = ((sa1, sa2, ma1, ma2), (sb1, sb2, mb1, mb2))
    items = [(sub, c) for sub in range(n_sub) for c in range(n_chunks)]
    scores(items[0], bufs[0], None)
    token = None
    for n, item in enumerate(items):
        if n + 1 < len(items):
            scores(items[n + 1], bufs[(n + 1) % 2], token)
        token = consume(item, bufs[n % 2])
        if item[1] == n_chunks - 1:
            finalize(item[0])


def _diff_attn(qkv3, lq1, lk1, lq2, lk2, subw, *, tq, tk, n_sub, lambda_init):
    b, seq, _ = qkv3.shape
    hd = 2 * DA_HEAD_DIM
    tb = n_sub * tq
    kern = functools.partial(_diff_attn_kernel, tq=tq, tk=tk, lambda_init=lambda_init)
    vec = lambda n: pl.BlockSpec((1, n), lambda bi, h, qi: (0, 0))
    blk = (2 * (2 * seq * hd * 2 + 2 * tb * hd * 2) + 2 * n_sub * (2 * tq * LANES * 4 + tq * hd * 4)
           + 6 * tq * tk * 4)
    return pl.pallas_call(
        kern,
        out_shape=jax.ShapeDtypeStruct((b, seq, DA_HEADS * hd), jnp.bfloat16),
        grid=(b, DA_HEADS, seq // tb),
        in_specs=[
            vec(DA_HEAD_DIM), vec(DA_HEAD_DIM), vec(DA_HEAD_DIM), vec(DA_HEAD_DIM), vec(hd),
            pl.BlockSpec((1, tb, hd), lambda bi, h, qi: (bi, qi, h)),
            pl.BlockSpec((1, seq, hd), lambda bi, h, qi: (bi, 0, DA_HEADS + h)),
            pl.BlockSpec((1, seq, hd), lambda bi, h, qi: (bi, 0, 2 * DA_HEADS + h)),
        ],
        out_specs=pl.BlockSpec((1, tb, hd), lambda bi, h, qi: (bi, qi, h)),
        scratch_shapes=[
            pltpu.VMEM((n_sub, 2, tq, LANES), jnp.float32),
            pltpu.VMEM((n_sub, 2, tq, LANES), jnp.float32),
            pltpu.VMEM((n_sub, 2, tq, hd), jnp.float32),
        ] + [pltpu.VMEM((tq, tk), jnp.float32)] * 4 + [pltpu.VMEM((tq, LANES), jnp.float32)] * 4,
        compiler_params=pltpu.CompilerParams(
            dimension_semantics=("parallel", "parallel", "arbitrary"),
            vmem_limit_bytes=_vmem_limit(blk)),
        name="diff_attn",
    )(lq1, lk1, lq2, lk2, subw, qkv3, qkv3, qkv3)


def _neigh_bias(rpb, rows):
    heads = rpb.shape[0]
    rpb = rpb.astype(jnp.float32) * LOG2E
    qc = np.arange(GRID_W)[:, None]
    kc = np.arange(GRID_W)[None, :]
    col_start = np.clip(qc - NA_KW // 2, 0, GRID_W - NA_KW)
    col_ok = (kc >= col_start) & (kc < col_start + NA_KW)
    dc = kc - qc + (NA_KW - 1)
    tcol = jnp.full((heads, 2 * NA_KH - 1, GRID_W, GRID_W), NEG_BIG, jnp.float32)
    for v in range(2 * NA_KW - 1):
        tcol = jnp.where((col_ok & (dc == v))[None, None], rpb[:, :, v][:, :, None, None], tcol)
    strip = jnp.transpose(tcol, (0, 2, 1, 3)).reshape(heads, GRID_W, (2 * NA_KH - 1) * GRID_W)
    masked = lambda n: jnp.full((heads, GRID_W, n * GRID_W), NEG_BIG, jnp.float32)
    nblk = rows // NA_QROWS
    tables = []
    for rb, key_row0 in ((0, 0), (1, 0), (nblk - 1, (nblk - 3) * NA_QROWS)):
        lines = []
        for qr in range(NA_QROWS):
            r = rb * NA_QROWS + qr
            row_start = min(max(r - NA_KH // 2, 0), rows - NA_KH)
            kr_lo = max(row_start - key_row0, 0)
            kr_hi = min(row_start + NA_KH - 1 - key_row0, NA_KROWS - 1)
            dr_lo = key_row0 + kr_lo - r + NA_KH - 1
            pieces = [strip[:, :, dr_lo * GRID_W:(dr_lo + kr_hi - kr_lo + 1) * GRID_W]]
            if kr_lo > 0:
                pieces.insert(0, masked(kr_lo))
            if kr_hi < NA_KROWS - 1:
                pieces.append(masked(NA_KROWS - 1 - kr_hi))
            lines.append(jnp.concatenate(pieces, axis=-1))
        tables.append(jnp.concatenate(lines, axis=1))
    return jnp.stack(tables)


def _neigh_attn_kernel(bias_ref, q_ref, k_ref, v_ref, o_ref):
    d = NA_HEAD_DIM
    heads = q_ref.shape[2] // d

    def scores(h, token):
        cols = slice(h * d, (h + 1) * d)
        q = q_ref[0, :, cols]
        k = k_ref[0, :, cols]
        if token is not None:
            k = k + token
        s = lax.dot_general(q, k, _NT, preferred_element_type=jnp.float32) + bias_ref[0, h]
        return s, jnp.max(s, axis=1, keepdims=True)

    def consume(h, s, m):
        cols = slice(h * d, (h + 1) * d)
        p = jnp.exp2(s - m)
        l = jnp.sum(p, axis=1, keepdims=True)
        o = jnp.dot(p.astype(v_ref.dtype), v_ref[0, :, cols], preferred_element_type=jnp.float32)
        o_ref[0, :, cols] = (o / l).astype(o_ref.dtype)
        return _zero_after(o, k_ref.dtype)

    nxt = scores(0, None)
    token = None
    for h in range(heads):
        cur = nxt
        if h + 1 < heads:
            nxt = scores(h + 1, token)
        token = consume(h, *cur)


def _neigh_attn(qkv3, bias):
    b, seq, _ = qkv3.shape
    width = NA_HEADS * NA_HEAD_DIM
    nq = NA_QROWS * GRID_W
    nk = NA_KROWS * GRID_W
    nblk = seq // nq
    col0 = 3 * DA_HEADS * 2 * DA_HEAD_DIM
    start = lambda rb: jnp.clip(rb - 1, 0, nblk - 3) * nq
    block_class = lambda rb: jnp.where(rb == 0, 0, jnp.where(rb == nblk - 1, 2, 1))
    kv_spec = lambda base: pl.BlockSpec((pl.Element(1), pl.Element(nk), pl.Element(width)),
                                        lambda bi, rb: (bi, start(rb), col0 + base * width))
    blk = 2 * (NA_HEADS * nq * nk * 4 + 2 * nq * width * 2 + 2 * nk * width * 2) + 8 * nq * nk * 4
    return pl.pallas_call(
        _neigh_attn_kernel,
        out_shape=jax.ShapeDtypeStruct((b, seq, width), jnp.bfloat16),
        grid=(b, nblk),
        in_specs=[
            pl.BlockSpec((1, NA_HEADS, nq, nk), lambda bi, rb: (block_class(rb), 0, 0, 0)),
            pl.BlockSpec((1, nq, width), lambda bi, rb: (bi, rb, col0 // width)),
            kv_spec(1), kv_spec(2),
        ],
        out_specs=pl.BlockSpec((1, nq, width), lambda bi, rb: (bi, rb, 0)),
        compiler_params=pltpu.CompilerParams(
            dimension_semantics=("parallel", "arbitrary"),
            vmem_limit_bytes=_vmem_limit(blk)),
        name="neigh_attn",
    )(bias, qkv3, qkv3, qkv3)


def _merge_kernel(x_ref, h_ref, gpost_ref, gnext_ref, oa_ref, on_ref, wga_ref, wgb_ref, wa_ref, wb_ref, wo_ref,
                  o_ref, hn_ref, y_ref):
    j = pl.program_id(1)

    def accumulate(first):
        h = h_ref[...]
        f32 = jnp.float32
        ga = 1.0 / (1.0 + jnp.exp(-jnp.dot(h, wga_ref[...], preferred_element_type=f32)))
        gb = 1.0 / (1.0 + jnp.exp(-jnp.dot(h, wgb_ref[...], preferred_element_type=f32)))
        a = jnp.dot(oa_ref[...], wa_ref[...], preferred_element_type=f32)
        b = jnp.dot(on_ref[...], wb_ref[...], preferred_element_type=f32)
        mixed = (ga * a + gb * b).astype(wo_ref.dtype)
        tn = mixed.shape[1]
        wo = wo_ref[pl.ds(pl.multiple_of(j * tn, tn), tn), :]
        part = jnp.dot(mixed, wo, preferred_element_type=f32)
        y_ref[...] = part if first else y_ref[...] + part

    pl.when(j == 0)(lambda: accumulate(True))
    pl.when(j > 0)(lambda: accumulate(False))

    @pl.when(j == pl.num_programs(1) - 1)
    def _():
        x1 = x_ref[...] + _rms(y_ref[...], gpost_ref[...])
        o_ref[...] = x1
        hn_ref[...] = _rms(x1, gnext_ref[...]).astype(hn_ref.dtype)


def _merge(x2, h2, gpost, gnext, oa2, on2, w_in_bf16, wa, wb, wo, *, tm, tn):
    m, d = x2.shape
    ca = oa2.shape[1]
    cb = on2.shape[1]
    gate_col0 = (w_in_bf16.shape[1] - 2 * d) // tn
    nj = d // tn
    blk = (2 * (2 * tm * d * 4 + 2 * tm * d * 2 + tm * (ca + cb) * 2 + (2 * d + ca + cb) * tn * 2)
           + d * d * 2 + tm * d * 4 + 6 * tm * tn * 4)
    return pl.pallas_call(
        _merge_kernel,
        out_shape=(jax.ShapeDtypeStruct((m, d), jnp.float32), jax.ShapeDtypeStruct((m, d), jnp.bfloat16)),
        grid=(m // tm, nj),
        in_specs=[
            pl.BlockSpec((tm, d), lambda i, j: (i, 0)),
            pl.BlockSpec((tm, d), lambda i, j: (i, 0)),
            pl.BlockSpec((1, d), lambda i, j: (0, 0)),
            pl.BlockSpec((1, d), lambda i, j: (0, 0)),
            pl.BlockSpec((tm, ca), lambda i, j: (i, 0)),
            pl.BlockSpec((tm, cb), lambda i, j: (i, 0)),
            pl.BlockSpec((d, tn), lambda i, j: (0, gate_col0 + j)),
            pl.BlockSpec((d, tn), lambda i, j: (0, gate_col0 + nj + j)),
            pl.BlockSpec((ca, tn), lambda i, j: (0, j)),
            pl.BlockSpec((cb, tn), lambda i, j: (0, j)),
            pl.BlockSpec((d, d), lambda i, j: (0, 0), pipeline_mode=pl.Buffered(1)),
        ],
        out_specs=(pl.BlockSpec((tm, d), lambda i, j: (i, 0)), pl.BlockSpec((tm, d), lambda i, j: (i, 0))),
        scratch_shapes=[pltpu.VMEM((tm, d), jnp.float32)],
        compiler_params=pltpu.CompilerParams(
            dimension_semantics=("parallel", "arbitrary"),
            vmem_limit_bytes=_vmem_limit(blk)),
        name="merge",
    )(x2, h2, gpost, gnext, oa2, on2, w_in_bf16, w_in_bf16, wa, wb, wo)


def _mlp_kernel(x_ref, h_ref, gpost_ref, wu_ref, wd_ref, o_ref, acc_ref):
    j = pl.program_id(1)

    def accumulate(first):
        u = jnp.dot(h_ref[...], wu_ref[...], preferred_element_type=jnp.float32)
        u = jnp.square(jnp.maximum(u, 0.0)).astype(wd_ref.dtype)
        part = jnp.dot(u, wd_ref[...], preferred_element_type=jnp.float32)
        acc_ref[...] = part if first else acc_ref[...] + part

    pl.when(j == 0)(lambda: accumulate(True))
    pl.when(j > 0)(lambda: accumulate(False))

    @pl.when(j == pl.num_programs(1) - 1)
    def _():
        o_ref[...] = x_ref[...] + _rms(acc_ref[...], gpost_ref[...])


def _mlp(x2, h2, gpost, wu, wd, *, tm, tf):
    m, d = x2.shape
    f = wu.shape[1]
    blk = 2 * (2 * tm * d * 4 + tm * d * 2 + 2 * d * tf * 2) + tm * d * 4 + 2 * tm * tf * 4
    return pl.pallas_call(
        _mlp_kernel,
        out_shape=jax.ShapeDtypeStruct((m, d), jnp.float32),
        grid=(m // tm, f // tf),
        in_specs=[
            pl.BlockSpec((tm, d), lambda i, j: (i, 0)),
            pl.BlockSpec((tm, d), lambda i, j: (i, 0)),
            pl.BlockSpec((1, d), lambda i, j: (0, 0)),
            pl.BlockSpec((d, tf), lambda i, j: (0, j)),
            pl.BlockSpec((tf, d), lambda i, j: (j, 0)),
        ],
        out_specs=pl.BlockSpec((tm, d), lambda i, j: (i, 0)),
        scratch_shapes=[pltpu.VMEM((tm, d), jnp.float32)],
        compiler_params=pltpu.CompilerParams(
            dimension_semantics=("parallel", "arbitrary"),
            vmem_limit_bytes=_vmem_limit(blk)),
        name="mlp",
    )(x2, h2, gpost, wu, wd)


def _rope_tables(seq):
    inv = (1.0 / (np.float32(ROPE_THETA) ** (np.arange(0, DA_HEAD_DIM, 2, dtype=np.float32) / DA_HEAD_DIM)))
    ang = np.arange(seq, dtype=np.float32)[:, None] * inv.astype(np.float32)[None, :]
    cos, sin = np.cos(ang).astype(np.float32), np.sin(ang).astype(np.float32)
    return jnp.asarray(np.concatenate([cos, cos], axis=1)), jnp.asarray(np.concatenate([-sin, sin], axis=1))


def kernel(x, w_in, w_branch_a, w_branch_b, w_out, norm_mix_pre, norm_mix_post, norm_mlp_pre, norm_mlp_post,
           lam_q1, lam_k1, lam_q2, lam_k2, subln_w, na_rpb, w_up, w_down):
    b, seq, d = x.shape
    m = b * seq
    depth = w_in.shape[0]
    da_width = DA_HEADS * 2 * DA_HEAD_DIM
    na_width = NA_HEADS * NA_HEAD_DIM
    qkv_cols = 3 * da_width + 3 * na_width
    bf16 = jnp.bfloat16
    cosd, sind = _rope_tables(seq)
    row = lambda v: v.reshape(1, -1).astype(jnp.float32)
    tm = min(512, seq)
    tq = min(512, seq)
    x2 = x.reshape(m, d)
    for l in range(depth):
        lambda_init = 0.8 - 0.6 * math.exp(-0.3 * l)
        w_in_l = w_in[l].astype(bf16)
        qkv, h_mix = _qkv_proj(x2, row(norm_mix_pre[l]), w_in_l, cosd, sind, seq,
                               tm=min(1024, seq), tn=2048, da_width=da_width, na_width=na_width)
        qkv3 = qkv.reshape(b, seq, qkv_cols)
        oa = _diff_attn(qkv3, row(lam_q1[l]), row(lam_k1[l]), row(lam_q2[l]), row(lam_k2[l]), row(subln_w[l]),
                        tq=tq, tk=min(1024, seq), n_sub=min(2, seq // tq), lambda_init=lambda_init)
        on = _neigh_attn(qkv3, _neigh_bias(na_rpb[l], seq // GRID_W))
        x2, h_mlp = _merge(x2, h_mix, row(norm_mix_post[l]), row(norm_mlp_pre[l]), oa.reshape(m, da_width),
                           on.reshape(m, na_width), w_in_l, w_branch_a[l].astype(bf16), w_branch_b[l].astype(bf16),
                           w_out[l].astype(bf16), tm=tm, tn=512)
        x2 = _mlp(x2, h_mlp, row(norm_mlp_post[l]), w_up[l].astype(bf16), w_down[l].astype(bf16),
                  tm=tm, tf=min(1024, w_up.shape[2]))
    return x2.reshape(b, seq, d)
```

```python
import functools
import math

import jax
import jax.numpy as jnp
import numpy as np
from jax import lax
from jax.experimental import pallas as pl
from jax.experimental.pallas import tpu as pltpu

EPS = 1e-6
ROPE_THETA = 10000.0
GRID_W = 64
DA_HEADS = 4
DA_HEAD_DIM = 128
NA_HEADS = 8
NA_HEAD_DIM = 128
NA_KH = 8
NA_KW = 16
LANES = 128
SUBLANES = 8
BF16_ROWS = 16
MXU_TILE = 256
NEG_BIG = -1e30
NA_QROWS = 4
NA_KROWS = 12
V7X_VMEM_BYTES = 64 * 1024 * 1024
assert DA_HEAD_DIM == NA_HEAD_DIM
LOG2E = math.log2(math.e)
QK_MULT = DA_HEAD_DIM ** -0.5 * LOG2E

_NT = (((1,), (1,)), ((), ()))


def _vmem_limit(block_bytes):
    return int(min(block_bytes + 16 * 1024 * 1024, V7X_VMEM_BYTES - 6 * 1024 * 1024))


def _rms(x, g):
    return x * lax.rsqrt(jnp.mean(x * x, axis=-1, keepdims=True) + EPS) * g


def _zero_after(x, dtype):
    bits = pltpu.bitcast(x[0:SUBLANES, 0:LANES], jnp.uint32)
    return pltpu.bitcast((bits >> 16) >> 16, jnp.float32)[0:1, :].astype(dtype)


def _qkv_kernel(x_ref, g_ref, w_ref, cos_ref, sin_ref, o_ref, h_ref, *, group_cols):
    j = pl.program_id(1)

    @pl.when(j == 0)
    def _():
        h_ref[...] = _rms(x_ref[...], g_ref[...]).astype(h_ref.dtype)

    acc = jnp.dot(h_ref[...], w_ref[...], preferred_element_type=jnp.float32)

    groups_per_tile = acc.shape[1] // group_cols
    for gi in range(groups_per_tile):
        group = j * groups_per_tile + gi
        rot = jnp.where(group == 0, QK_MULT, jnp.where(group == 1, 1.0, 0.0)).astype(jnp.float32)
        flat = jnp.where(group == 3, QK_MULT, jnp.where(group <= 1, 0.0, 1.0)).astype(jnp.float32)
        cs = cos_ref[...] * rot + flat
        sn = sin_ref[...] * rot
        for g in range(gi * group_cols // LANES, (gi + 1) * group_cols // LANES):
            y = acc[:, g * LANES:(g + 1) * LANES]
            o_ref[:, g * LANES:(g + 1) * LANES] = (y * cs + pltpu.roll(y, LANES // 2, 1) * sn).astype(o_ref.dtype)


def _qkv_proj(x2, gain, w_bf16, cosd, sind, seq, *, tm, tn, da_width, na_width):
    m, d = x2.shape
    assert da_width == na_width and tn % da_width == 0
    n_cols = 3 * da_width + 3 * na_width
    pos_blocks = seq // tm
    blk = 2 * (tm * d * 4 + d * tn * 2 + tm * tn * 2 + 2 * tm * LANES * 4 + tm * d * 2) + tm * tn * 4
    return pl.pallas_call(
        functools.partial(_qkv_kernel, group_cols=da_width),
        out_shape=(jax.ShapeDtypeStruct((m, n_cols), jnp.bfloat16), jax.ShapeDtypeStruct((m, d), jnp.bfloat16)),
        grid=(m // tm, n_cols // tn),
        in_specs=[
            pl.BlockSpec((tm, d), lambda i, j: (i, 0)),
            pl.BlockSpec((1, d), lambda i, j: (0, 0)),
            pl.BlockSpec((d, tn), lambda i, j: (0, j)),
            pl.BlockSpec((tm, LANES), lambda i, j: (i % pos_blocks, 0)),
            pl.BlockSpec((tm, LANES), lambda i, j: (i % pos_blocks, 0)),
        ],
        out_specs=(pl.BlockSpec((tm, tn), lambda i, j: (i, j)), pl.BlockSpec((tm, d), lambda i, j: (i, 0))),
        compiler_params=pltpu.CompilerParams(
            dimension_semantics=("parallel", "arbitrary"),
            vmem_limit_bytes=_vmem_limit(blk)),
        name="qkv_proj",
    )(x2, gain, w_bf16, cosd, sind)


def _diff_attn_kernel(*refs, tq, tk, lambda_init, n_cast):
    lq1_ref, lk1_ref, lq2_ref, lk2_ref, subw_ref, q_ref, k_ref, v_ref = refs[:8]
    cast_in = refs[8:8 + n_cast]
    o_ref = refs[8 + n_cast]
    cast_out = refs[9 + n_cast:9 + 2 * n_cast]
    m_ref, l_ref, a_ref, sa1, sa2, sb1, sb2, ma1, ma2, mb1, mb2 = refs[9 + 2 * n_cast:]
    for w_ref, wo_ref in zip(cast_in, cast_out):
        wo_ref[...] = w_ref[...].astype(wo_ref.dtype)
    seq = k_ref.shape[1]
    n_chunks = seq // tk
    n_sub = q_ref.shape[1] // tq
    d = DA_HEAD_DIM
    lam = (jnp.exp(jnp.sum(lq1_ref[...] * lk1_ref[...])) - jnp.exp(jnp.sum(lq2_ref[...] * lk2_ref[...]))
           + lambda_init)

    def scores(item, dst, token):
        sub, c = item
        for half in range(2):
            q = q_ref[0, sub * tq:(sub + 1) * tq, half * d:(half + 1) * d]
            mx = None
            for t in range(tk // MXU_TILE):
                k = k_ref[0, pl.ds(c * tk + t * MXU_TILE, MXU_TILE), half * d:(half + 1) * d]
                if token is not None:
                    k = k + token
                s = lax.dot_general(q, k, _NT, preferred_element_type=jnp.float32)
                dst[half][:, t * MXU_TILE:(t + 1) * MXU_TILE] = s
                for u in range(MXU_TILE // LANES):
                    su = s[:, u * LANES:(u + 1) * LANES]
                    mx = su if mx is None else jnp.maximum(mx, su)
            dst[2 + half][...] = jnp.broadcast_to(jnp.max(mx, axis=1, keepdims=True), mx.shape)

    def consume(item, src):
        sub, c = item
        token = None
        for half in range(2):
            if c == 0:
                m_prev = jnp.full((tq, LANES), NEG_BIG, jnp.float32)
            else:
                m_prev = m_ref[sub, half]
            m_new = jnp.maximum(m_prev, src[2 + half][...])
            alpha = jnp.exp2(m_prev - m_new)
            m_rep = jnp.concatenate([m_new] * (MXU_TILE // LANES), axis=1)
            psum = None
            pv = None
            for t in range(tk // MXU_TILE):
                p = jnp.exp2(src[half][:, t * MXU_TILE:(t + 1) * MXU_TILE] - m_rep)
                for u in range(MXU_TILE // LANES):
                    pu = p[:, u * LANES:(u + 1) * LANES]
                    psum = pu if psum is None else psum + pu
                v = v_ref[0, pl.ds(c * tk + t * MXU_TILE, MXU_TILE), :]
                pvt = jnp.dot(p.astype(v.dtype), v, preferred_element_type=jnp.float32)
                pv = pvt if pv is None else pv + pvt
            if c == 0:
                l_ref[sub, half] = psum
                a_ref[sub, half] = pv
            else:
                l_ref[sub, half] = alpha * l_ref[sub, half] + psum
                a_ref[sub, half] = a_ref[sub, half] * jnp.concatenate([alpha] * (2 * d // LANES), axis=1) + pv
            m_ref[sub, half] = m_new
            if half == 0:
                token = _zero_after(pv, k_ref.dtype)
        return token

    def finalize(sub):
        o1 = a_ref[sub, 0] / jnp.sum(l_ref[sub, 0], axis=1, keepdims=True)
        o2 = a_ref[sub, 1] / jnp.sum(l_ref[sub, 1], axis=1, keepdims=True)
        o = _rms(o1 - lam * o2, subw_ref[...]) * (1.0 - lambda_init)
        o_ref[0, sub * tq:(sub + 1) * tq, :] = o.astype(o_ref.dtype)

    bufs = ((sa1, sa2, ma1, ma2), (sb1, sb2, mb1, mb2))
    items = [(sub, c) for sub in range(n_sub) for c in range(n_chunks)]
    scores(items[0], bufs[0], None)
    token = None
    for n, item in enumerate(items):
        if n + 1 < len(items):
            scores(items[n + 1], bufs[(n + 1) % 2], token)
        token = consume(item, bufs[n % 2])
        if item[1] == n_chunks - 1:
            finalize(item[0])


def _cast_rows_per_step(rows, steps):
    r = max(BF16_ROWS, rows // steps)
    return r if rows % r == 0 and rows // r <= steps else None


def _diff_attn(qkv3, lq1, lk1, lq2, lk2, subw, cast, *, tq, tk, n_sub, lambda_init):
    b, seq, _ = qkv3.shape
    hd = 2 * DA_HEAD_DIM
    tb = n_sub * tq
    nq = seq // tb
    steps = b * DA_HEADS * nq
    kern = functools.partial(_diff_attn_kernel, tq=tq, tk=tk, lambda_init=lambda_init, n_cast=len(cast))
    vec = lambda n: pl.BlockSpec((1, n), lambda bi, h, qi: (0, 0))

    def cast_spec(w):
        r = _cast_rows_per_step(w.shape[0], steps)
        nb = w.shape[0] // r
        return pl.BlockSpec((r, w.shape[1]), lambda bi, h, qi: (((bi * DA_HEADS + h) * nq + qi) * nb // steps, 0))

    cast_specs = [cast_spec(w) for w in cast]
    blk = (2 * (2 * seq * hd * 2 + 2 * tb * hd * 2) + 2 * n_sub * (2 * tq * LANES * 4 + tq * hd * 4)
           + 6 * tq * tk * 4 + sum(2 * sp.block_shape[0] * sp.block_shape[1] * 6 for sp in cast_specs))
    outs = pl.pallas_call(
        kern,
        out_shape=[jax.ShapeDtypeStruct((b, seq, DA_HEADS * hd), jnp.bfloat16)]
        + [jax.ShapeDtypeStruct(w.shape, jnp.bfloat16) for w in cast],
        grid=(b, DA_HEADS, nq),
        in_specs=[
            vec(DA_HEAD_DIM), vec(DA_HEAD_DIM), vec(DA_HEAD_DIM), vec(DA_HEAD_DIM), vec(hd),
            pl.BlockSpec((1, tb, hd), lambda bi, h, qi: (bi, qi, h)),
            pl.BlockSpec((1, seq, hd), lambda bi, h, qi: (bi, 0, DA_HEADS + h)),
            pl.BlockSpec((1, seq, hd), lambda bi, h, qi: (bi, 0, 2 * DA_HEADS + h)),
        ] + cast_specs,
        out_specs=[pl.BlockSpec((1, tb, hd), lambda bi, h, qi: (bi, qi, h))] + cast_specs,
        scratch_shapes=[
            pltpu.VMEM((n_sub, 2, tq, LANES), jnp.float32),
            pltpu.VMEM((n_sub, 2, tq, LANES), jnp.float32),
            pltpu.VMEM((n_sub, 2, tq, hd), jnp.float32),
        ] + [pltpu.VMEM((tq, tk), jnp.float32)] * 4 + [pltpu.VMEM((tq, LANES), jnp.float32)] * 4,
        compiler_params=pltpu.CompilerParams(
            dimension_semantics=("arbitrary", "arbitrary", "arbitrary"),
            vmem_limit_bytes=_vmem_limit(blk)),
        name="diff_attn",
    )(lq1, lk1, lq2, lk2, subw, qkv3, qkv3, qkv3, *cast)
    return outs[0], outs[1:]


def _neigh_bias(rpb, rows):
    heads = rpb.shape[0]
    rpb = rpb.astype(jnp.float32) * LOG2E
    qc = np.arange(GRID_W)[:, None]
    kc = np.arange(GRID_W)[None, :]
    col_start = np.clip(qc - NA_KW // 2, 0, GRID_W - NA_KW)
    col_ok = (kc >= col_start) & (kc < col_start + NA_KW)
    dc = kc - qc + (NA_KW - 1)
    tcol = jnp.full((heads, 2 * NA_KH - 1, GRID_W, GRID_W), NEG_BIG, jnp.float32)
    for v in range(2 * NA_KW - 1):
        tcol = jnp.where((col_ok & (dc == v))[None, None], rpb[:, :, v][:, :, None, None], tcol)
    strip = jnp.transpose(tcol, (0, 2, 1, 3)).reshape(heads, GRID_W, (2 * NA_KH - 1) * GRID_W)
    masked = lambda n: jnp.full((heads, GRID_W, n * GRID_W), NEG_BIG, jnp.float32)
    nblk = rows // NA_QROWS
    tables = []
    for rb, key_row0 in ((0, 0), (1, 0), (nblk - 1, (nblk - 3) * NA_QROWS)):
        lines = []
        for qr in range(NA_QROWS):
            r = rb * NA_QROWS + qr
            row_start = min(max(r - NA_KH // 2, 0), rows - NA_KH)
            kr_lo = max(row_start - key_row0, 0)
            kr_hi = min(row_start + NA_KH - 1 - key_row0, NA_KROWS - 1)
            dr_lo = key_row0 + kr_lo - r + NA_KH - 1
            pieces = [strip[:, :, dr_lo * GRID_W:(dr_lo + kr_hi - kr_lo + 1) * GRID_W]]
            if kr_lo > 0:
                pieces.insert(0, masked(kr_lo))
            if kr_hi < NA_KROWS - 1:
                pieces.append(masked(NA_KROWS - 1 - kr_hi))
            lines.append(jnp.concatenate(pieces, axis=-1))
        tables.append(jnp.concatenate(lines, axis=1))
    return jnp.stack(tables)


def _neigh_attn_kernel(bias_ref, q_ref, k_ref, v_ref, o_ref):
    d = NA_HEAD_DIM
    heads = q_ref.shape[2] // d

    def scores(h, token):
        cols = slice(h * d, (h + 1) * d)
        q = q_ref[0, :, cols]
        k = k_ref[0, :, cols]
        if token is not None:
            k = k + token
        s = lax.dot_general(q, k, _NT, preferred_element_type=jnp.float32) + bias_ref[0, h]
        return s, jnp.max(s, axis=1, keepdims=True)

    def consume(h, s, m):
        cols = slice(h * d, (h + 1) * d)
        p = jnp.exp2(s - m)
        l = jnp.sum(p, axis=1, keepdims=True)
        o = jnp.dot(p.astype(v_ref.dtype), v_ref[0, :, cols], preferred_element_type=jnp.float32)
        o_ref[0, :, cols] = (o / l).astype(o_ref.dtype)
        return _zero_after(o, k_ref.dtype)

    nxt = scores(0, None)
    token = None
    for h in range(heads):
        cur = nxt
        if h + 1 < heads:
            nxt = scores(h + 1, token)
        token = consume(h, *cur)


def _neigh_attn(qkv3, bias):
    b, seq, _ = qkv3.shape
    width = NA_HEADS * NA_HEAD_DIM
    nq = NA_QROWS * GRID_W
    nk = NA_KROWS * GRID_W
    nblk = seq // nq
    col0 = 3 * DA_HEADS * 2 * DA_HEAD_DIM
    start = lambda rb: jnp.clip(rb - 1, 0, nblk - 3) * nq
    block_class = lambda rb: jnp.where(rb == 0, 0, jnp.where(rb == nblk - 1, 2, 1))
    kv_spec = lambda base: pl.BlockSpec((pl.Element(1), pl.Element(nk), pl.Element(width)),
                                        lambda bi, rb: (bi, start(rb), col0 + base * width))
    blk = 2 * (NA_HEADS * nq * nk * 4 + 2 * nq * width * 2 + 2 * nk * width * 2) + 8 * nq * nk * 4
    return pl.pallas_call(
        _neigh_attn_kernel,
        out_shape=jax.ShapeDtypeStruct((b, seq, width), jnp.bfloat16),
        grid=(b, nblk),
        in_specs=[
            pl.BlockSpec((1, NA_HEADS, nq, nk), lambda bi, rb: (block_class(rb), 0, 0, 0)),
            pl.BlockSpec((1, nq, width), lambda bi, rb: (bi, rb, col0 // width)),
            kv_spec(1), kv_spec(2),
        ],
        out_specs=pl.BlockSpec((1, nq, width), lambda bi, rb: (bi, rb, 0)),
        compiler_params=pltpu.CompilerParams(
            dimension_semantics=("parallel", "arbitrary"),
            vmem_limit_bytes=_vmem_limit(blk)),
        name="neigh_attn",
    )(bias, qkv3, qkv3, qkv3)


def _merge_kernel(x_ref, h_ref, gpost_ref, gnext_ref, oa_ref, on_ref, wga_ref, wgb_ref, wa_ref, wb_ref, wo_ref,
                  o_ref, hn_ref, y_ref):
    j = pl.program_id(1)

    def accumulate(first):
        h = h_ref[...]
        f32 = jnp.float32
        ga = 1.0 / (1.0 + jnp.exp(-jnp.dot(h, wga_ref[...], preferred_element_type=f32)))
        gb = 1.0 / (1.0 + jnp.exp(-jnp.dot(h, wgb_ref[...], preferred_element_type=f32)))
        a = jnp.dot(oa_ref[...], wa_ref[...], preferred_element_type=f32)
        b = jnp.dot(on_ref[...], wb_ref[...], preferred_element_type=f32)
        mixed = (ga * a + gb * b).astype(wo_ref.dtype)
        tn = mixed.shape[1]
        wo = wo_ref[pl.ds(pl.multiple_of(j * tn, tn), tn), :]
        part = jnp.dot(mixed, wo, preferred_element_type=f32)
        y_ref[...] = part if first else y_ref[...] + part

    pl.when(j == 0)(lambda: accumulate(True))
    pl.when(j > 0)(lambda: accumulate(False))

    @pl.when(j == pl.num_programs(1) - 1)
    def _():
        x1 = x_ref[...] + _rms(y_ref[...], gpost_ref[...])
        o_ref[...] = x1
        hn_ref[...] = _rms(x1, gnext_ref[...]).astype(hn_ref.dtype)


def _merge(x2, h2, gpost, gnext, oa2, on2, w_in_bf16, wa, wb, wo, *, tm, tn):
    m, d = x2.shape
    ca = oa2.shape[1]
    cb = on2.shape[1]
    gate_col0 = (w_in_bf16.shape[1] - 2 * d) // tn
    nj = d // tn
    blk = (2 * (2 * tm * d * 4 + 2 * tm * d * 2 + tm * (ca + cb) * 2 + (2 * d + ca + cb) * tn * 2)
           + d * d * 2 + tm * d * 4 + 6 * tm * tn * 4)
    return pl.pallas_call(
        _merge_kernel,
        out_shape=(jax.ShapeDtypeStruct((m, d), jnp.float32), jax.ShapeDtypeStruct((m, d), jnp.bfloat16)),
        grid=(m // tm, nj),
        in_specs=[
            pl.BlockSpec((tm, d), lambda i, j: (i, 0)),
            pl.BlockSpec((tm, d), lambda i, j: (i, 0)),
            pl.BlockSpec((1, d), lambda i, j: (0, 0)),
            pl.BlockSpec((1, d), lambda i, j: (0, 0)),
            pl.BlockSpec((tm, ca), lambda i, j: (i, 0)),
            pl.BlockSpec((tm, cb), lambda i, j: (i, 0)),
            pl.BlockSpec((d, tn), lambda i, j: (0, gate_col0 + j)),
            pl.BlockSpec((d, tn), lambda i, j: (0, gate_col0 + nj + j)),
            pl.BlockSpec((ca, tn), lambda i, j: (0, j)),
            pl.BlockSpec((cb, tn), lambda i, j: (0, j)),
            pl.BlockSpec((d, d), lambda i, j: (0, 0), pipeline_mode=pl.Buffered(1)),
        ],
        out_specs=(pl.BlockSpec((tm, d), lambda i, j: (i, 0)), pl.BlockSpec((tm, d), lambda i, j: (i, 0))),
        scratch_shapes=[pltpu.VMEM((tm, d), jnp.float32)],
        compiler_params=pltpu.CompilerParams(
            dimension_semantics=("parallel", "arbitrary"),
            vmem_limit_bytes=_vmem_limit(blk)),
        name="merge",
    )(x2, h2, gpost, gnext, oa2, on2, w_in_bf16, w_in_bf16, wa, wb, wo)


def _mlp_kernel(x_ref, h_ref, gpost_ref, wu_ref, wd_ref, o_ref, acc_ref):
    j = pl.program_id(1)

    def accumulate(first):
        u = jnp.dot(h_ref[...], wu_ref[...], preferred_element_type=jnp.float32)
        u = jnp.square(jnp.maximum(u, 0.0)).astype(wd_ref.dtype)
        part = jnp.dot(u, wd_ref[...], preferred_element_type=jnp.float32)
        acc_ref[...] = part if first else acc_ref[...] + part

    pl.when(j == 0)(lambda: accumulate(True))
    pl.when(j > 0)(lambda: accumulate(False))

    @pl.when(j == pl.num_programs(1) - 1)
    def _():
        o_ref[...] = x_ref[...] + _rms(acc_ref[...], gpost_ref[...])


def _mlp(x2, h2, gpost, wu, wd, *, tm, tf):
    m, d = x2.shape
    f = wu.shape[1]
    blk = 2 * (2 * tm * d * 4 + tm * d * 2 + 2 * d * tf * 2) + tm * d * 4 + 2 * tm * tf * 4
    return pl.pallas_call(
        _mlp_kernel,
        out_shape=jax.ShapeDtypeStruct((m, d), jnp.float32),
        grid=(m // tm, f // tf),
        in_specs=[
            pl.BlockSpec((tm, d), lambda i, j: (i, 0)),
            pl.BlockSpec((tm, d), lambda i, j: (i, 0)),
            pl.BlockSpec((1, d), lambda i, j: (0, 0)),
            pl.BlockSpec((d, tf), lambda i, j: (0, j)),
            pl.BlockSpec((tf, d), lambda i, j: (j, 0)),
        ],
        out_specs=pl.BlockSpec((tm, d), lambda i, j: (i, 0)),
        scratch_shapes=[pltpu.VMEM((tm, d), jnp.float32)],
        compiler_params=pltpu.CompilerParams(
            dimension_semantics=("parallel", "arbitrary"),
            vmem_limit_bytes=_vmem_limit(blk)),
        name="mlp",
    )(x2, h2, gpost, wu, wd)


def _rope_tables(seq):
    inv = (1.0 / (np.float32(ROPE_THETA) ** (np.arange(0, DA_HEAD_DIM, 2, dtype=np.float32) / DA_HEAD_DIM)))
    ang = np.arange(seq, dtype=np.float32)[:, None] * inv.astype(np.float32)[None, :]
    cos, sin = np.cos(ang).astype(np.float32), np.sin(ang).astype(np.float32)
    return jnp.asarray(np.concatenate([cos, cos], axis=1)), jnp.asarray(np.concatenate([-sin, sin], axis=1))


def kernel(x, w_in, w_branch_a, w_branch_b, w_out, norm_mix_pre, norm_mix_post, norm_mlp_pre, norm_mlp_post,
           lam_q1, lam_k1, lam_q2, lam_k2, subln_w, na_rpb, w_up, w_down):
    b, seq, d = x.shape
    m = b * seq
    depth = w_in.shape[0]
    da_width = DA_HEADS * 2 * DA_HEAD_DIM
    na_width = NA_HEADS * NA_HEAD_DIM
    qkv_cols = 3 * da_width + 3 * na_width
    bf16 = jnp.bfloat16
    cosd, sind = _rope_tables(seq)
    row = lambda v: v.reshape(1, -1).astype(jnp.float32)
    tm = min(512, seq)
    tq = min(512, seq)
    n_sub = min(2, seq // tq)
    x2 = x.reshape(m, d)
    for l in range(depth):
        lambda_init = 0.8 - 0.6 * math.exp(-0.3 * l)
        w_in_l = w_in[l].astype(bf16)
        qkv, h_mix = _qkv_proj(x2, row(norm_mix_pre[l]), w_in_l, cosd, sind, seq,
                               tm=min(1024, seq), tn=2048, da_width=da_width, na_width=na_width)
        qkv3 = qkv.reshape(b, seq, qkv_cols)
        late = [w_branch_a[l], w_branch_b[l], w_out[l], w_up[l].reshape(-1, d), w_down[l]]
        steps = b * DA_HEADS * (seq // (tq * n_sub))
        inside = [w.shape[1] == d and _cast_rows_per_step(w.shape[0], steps) is not None for w in late]
        oa, done = _diff_attn(qkv3, row(lam_q1[l]), row(lam_k1[l]), row(lam_q2[l]), row(lam_k2[l]), row(subln_w[l]),
                              [w for w, ok in zip(late, inside) if ok],
                              tq=tq, tk=min(1024, seq), n_sub=n_sub, lambda_init=lambda_init)
        done = list(done)
        wa, wb, wo, wu, wd = [done.pop(0) if ok else w.astype(bf16) for w, ok in zip(late, inside)]
        wu = wu.reshape(w_up[l].shape)
        on = _neigh_attn(qkv3, _neigh_bias(na_rpb[l], seq // GRID_W))
        x2, h_mlp = _merge(x2, h_mix, row(norm_mix_post[l]), row(norm_mlp_pre[l]), oa.reshape(m, da_width),
                           on.reshape(m, na_width), w_in_l, wa, wb, wo, tm=tm, tn=512)
        x2 = _mlp(x2, h_mlp, row(norm_mlp_post[l]), wu, wd, tm=tm, tf=min(1024, w_up.shape[2]))
    return x2.reshape(b, seq, d)
```

```python
import functools
import math

import jax
import jax.numpy as jnp
import numpy as np
from jax import lax
from jax.experimental import pallas as pl
from jax.experimental.pallas import tpu as pltpu

EPS = 1e-6
ROPE_THETA = 10000.0
GRID_W = 64
DA_HEADS = 4
DA_HEAD_DIM = 128
NA_HEADS = 8
NA_HEAD_DIM = 128
NA_KH = 8
NA_KW = 16
LANES = 128
SUBLANES = 8
BF16_ROWS = 16
MXU_TILE = 256
NEG_BIG = -1e30
NA_QROWS = 4
NA_KROWS = 12
V7X_VMEM_BYTES = 64 * 1024 * 1024
assert DA_HEAD_DIM == NA_HEAD_DIM
LOG2E = math.log2(math.e)
QK_MULT = DA_HEAD_DIM ** -0.5 * LOG2E

_NT = (((1,), (1,)), ((), ()))


def _vmem_limit(block_bytes):
    return int(min(block_bytes + 16 * 1024 * 1024, V7X_VMEM_BYTES - 6 * 1024 * 1024))


def _rms(x, g):
    return x * lax.rsqrt(jnp.mean(x * x, axis=-1, keepdims=True) + EPS) * g


def _zero_after(x, dtype):
    bits = pltpu.bitcast(x[0:SUBLANES, 0:LANES], jnp.uint32)
    return pltpu.bitcast((bits >> 16) >> 16, jnp.float32)[0:1, :].astype(dtype)


def _qkv_kernel(x_ref, g_ref, w_ref, cos_ref, sin_ref, o_ref, h_ref, *, group_cols):
    j = pl.program_id(1)

    @pl.when(j == 0)
    def _():
        h_ref[...] = _rms(x_ref[...], g_ref[...]).astype(h_ref.dtype)

    acc = jnp.dot(h_ref[...], w_ref[...], preferred_element_type=jnp.float32)

    groups_per_tile = acc.shape[1] // group_cols
    for gi in range(groups_per_tile):
        group = j * groups_per_tile + gi
        rot = jnp.where(group == 0, QK_MULT, jnp.where(group == 1, 1.0, 0.0)).astype(jnp.float32)
        flat = jnp.where(group == 3, QK_MULT, jnp.where(group <= 1, 0.0, 1.0)).astype(jnp.float32)
        cs = cos_ref[...] * rot + flat
        sn = sin_ref[...] * rot
        for g in range(gi * group_cols // LANES, (gi + 1) * group_cols // LANES):
            y = acc[:, g * LANES:(g + 1) * LANES]
            o_ref[:, g * LANES:(g + 1) * LANES] = (y * cs + pltpu.roll(y, LANES // 2, 1) * sn).astype(o_ref.dtype)


def _qkv_proj(x2, gain, w_bf16, cosd, sind, seq, *, tm, tn, da_width, na_width):
    m, d = x2.shape
    assert da_width == na_width and tn % da_width == 0
    n_cols = 3 * da_width + 3 * na_width
    pos_blocks = seq // tm
    blk = 2 * (tm * d * 4 + d * tn * 2 + tm * tn * 2 + 2 * tm * LANES * 4 + tm * d * 2) + tm * tn * 4
    return pl.pallas_call(
        functools.partial(_qkv_kernel, group_cols=da_width),
        out_shape=(jax.ShapeDtypeStruct((m, n_cols), jnp.bfloat16), jax.ShapeDtypeStruct((m, d), jnp.bfloat16)),
        grid=(m // tm, n_cols // tn),
        in_specs=[
            pl.BlockSpec((tm, d), lambda i, j: (i, 0)),
            pl.BlockSpec((1, d), lambda i, j: (0, 0)),
            pl.BlockSpec((d, tn), lambda i, j: (0, j)),
            pl.BlockSpec((tm, LANES), lambda i, j: (i % pos_blocks, 0)),
            pl.BlockSpec((tm, LANES), lambda i, j: (i % pos_blocks, 0)),
        ],
        out_specs=(pl.BlockSpec((tm, tn), lambda i, j: (i, j)), pl.BlockSpec((tm, d), lambda i, j: (i, 0))),
        compiler_params=pltpu.CompilerParams(
            dimension_semantics=("parallel", "arbitrary"),
            vmem_limit_bytes=_vmem_limit(blk)),
        name="qkv_proj",
    )(x2, gain, w_bf16, cosd, sind)


def _diff_attn_kernel(*refs, tq, tk, lambda_init, n_cast):
    lq1_ref, lk1_ref, lq2_ref, lk2_ref, subw_ref, q_ref, k_ref, v_ref = refs[:8]
    cast_in = refs[8:8 + n_cast]
    o_ref = refs[8 + n_cast]
    cast_out = refs[9 + n_cast:9 + 2 * n_cast]
    m_ref, l_ref, a_ref, sa1, sa2, sb1, sb2, ma1, ma2, mb1, mb2 = refs[9 + 2 * n_cast:]
    for w_ref, wo_ref in zip(cast_in, cast_out):
        wo_ref[...] = w_ref[...].astype(wo_ref.dtype)
    seq = k_ref.shape[1]
    n_chunks = seq // tk
    n_sub = q_ref.shape[1] // tq
    d = DA_HEAD_DIM
    lam = (jnp.exp(jnp.sum(lq1_ref[...] * lk1_ref[...])) - jnp.exp(jnp.sum(lq2_ref[...] * lk2_ref[...]))
           + lambda_init)

    def scores(item, dst, token):
        sub, c = item
        for half in range(2):
            q = q_ref[0, sub * tq:(sub + 1) * tq, half * d:(half + 1) * d]
            mx = None
            for t in range(tk // MXU_TILE):
                k = k_ref[0, pl.ds(c * tk + t * MXU_TILE, MXU_TILE), half * d:(half + 1) * d]
                if token is not None:
                    k = k + token
                s = lax.dot_general(q, k, _NT, preferred_element_type=jnp.float32)
                dst[half][:, t * MXU_TILE:(t + 1) * MXU_TILE] = s
                for u in range(MXU_TILE // LANES):
                    su = s[:, u * LANES:(u + 1) * LANES]
                    mx = su if mx is None else jnp.maximum(mx, su)
            dst[2 + half][...] = jnp.broadcast_to(jnp.max(mx, axis=1, keepdims=True), mx.shape)

    def consume(item, src):
        sub, c = item
        token = None
        for half in range(2):
            if c == 0:
                m_prev = jnp.full((tq, LANES), NEG_BIG, jnp.float32)
            else:
                m_prev = m_ref[sub, half]
            m_new = jnp.maximum(m_prev, src[2 + half][...])
            alpha = jnp.exp2(m_prev - m_new)
            m_rep = jnp.concatenate([m_new] * (MXU_TILE // LANES), axis=1)
            psum = None
            pv = None
            for t in range(tk // MXU_TILE):
                p = jnp.exp2(src[half][:, t * MXU_TILE:(t + 1) * MXU_TILE] - m_rep)
                for u in range(MXU_TILE // LANES):
                    pu = p[:, u * LANES:(u + 1) * LANES]
                    psum = pu if psum is None else psum + pu
                v = v_ref[0, pl.ds(c * tk + t * MXU_TILE, MXU_TILE), :]
                pvt = jnp.dot(p.astype(v.dtype), v, preferred_element_type=jnp.float32)
                pv = pvt if pv is None else pv + pvt
            if c == 0:
                l_ref[sub, half] = psum
                a_ref[sub, half] = pv
            else:
                l_ref[sub, half] = alpha * l_ref[sub, half] + psum
                a_ref[sub, half] = a_ref[sub, half] * jnp.concatenate([alpha] * (2 * d // LANES), axis=1) + pv
            m_ref[sub, half] = m_new
            if half == 0:
                token = _zero_after(pv, k_ref.dtype)
        return token

    def finalize(sub):
        o1 = a_ref[sub, 0] / jnp.sum(l_ref[sub, 0], axis=1, keepdims=True)
        o2 = a_ref[sub, 1] / jnp.sum(l_ref[sub, 1], axis=1, keepdims=True)
        o = _rms(o1 - lam * o2, subw_ref[...]) * (1.0 - lambda_init)
        o_ref[0, sub * tq:(sub + 1) * tq, :] = o.astype(o_ref.dtype)

    bufs = ((sa1, sa2, ma1, ma2), (sb1, sb2, mb1, mb2))
    items = [(sub, c) for sub in range(n_sub) for c in range(n_chunks)]
    scores(items[0], bufs[0], None)
    token = None
    for n, item in enumerate(items):
        if n + 1 < len(items):
            scores(items[n + 1], bufs[(n + 1) % 2], token)
        token = consume(item, bufs[n % 2])
        if item[1] == n_chunks - 1:
            finalize(item[0])


def _cast_rows_per_step(rows, steps):
    r = max(BF16_ROWS, rows // steps)
    return r if rows % r == 0 and rows // r <= steps else None


def _diff_attn(qkv3, lq1, lk1, lq2, lk2, subw, cast, *, tq, tk, n_sub, lambda_init):
    b, seq, _ = qkv3.shape
    hd = 2 * DA_HEAD_DIM
    tb = n_sub * tq
    nq = seq // tb
    steps = b * DA_HEADS * nq
    kern = functools.partial(_diff_attn_kernel, tq=tq, tk=tk, lambda_init=lambda_init, n_cast=len(cast))
    vec = lambda n: pl.BlockSpec((1, n), lambda bi, h, qi: (0, 0))

    def cast_spec(w):
        r = _cast_rows_per_step(w.shape[0], steps)
        nb = w.shape[0] // r
        return pl.BlockSpec((r, w.shape[1]), lambda bi, h, qi: (((bi * DA_HEADS + h) * nq + qi) * nb // steps, 0))

    cast_specs = [cast_spec(w) for w in cast]
    blk = (2 * (2 * seq * hd * 2 + 2 * tb * hd * 2) + 2 * n_sub * (2 * tq * LANES * 4 + tq * hd * 4)
           + 6 * tq * tk * 4 + sum(2 * sp.block_shape[0] * sp.block_shape[1] * 6 for sp in cast_specs))
    outs = pl.pallas_call(
        kern,
        out_shape=[jax.ShapeDtypeStruct((b, seq, DA_HEADS * hd), jnp.bfloat16)]
        + [jax.ShapeDtypeStruct(w.shape, jnp.bfloat16) for w in cast],
        grid=(b, DA_HEADS, nq),
        in_specs=[
            vec(DA_HEAD_DIM), vec(DA_HEAD_DIM), vec(DA_HEAD_DIM), vec(DA_HEAD_DIM), vec(hd),
            pl.BlockSpec((1, tb, hd), lambda bi, h, qi: (bi, qi, h)),
            pl.BlockSpec((1, seq, hd), lambda bi, h, qi: (bi, 0, DA_HEADS + h)),
            pl.BlockSpec((1, seq, hd), lambda bi, h, qi: (bi, 0, 2 * DA_HEADS + h)),
        ] + cast_specs,
        out_specs=[pl.BlockSpec((1, tb, hd), lambda bi, h, qi: (bi, qi, h))] + cast_specs,
        scratch_shapes=[
            pltpu.VMEM((n_sub, 2, tq, LANES), jnp.float32),
            pltpu.VMEM((n_sub, 2, tq, LANES), jnp.float32),
            pltpu.VMEM((n_sub, 2, tq, hd), jnp.float32),
        ] + [pltpu.VMEM((tq, tk), jnp.float32)] * 4 + [pltpu.VMEM((tq, LANES), jnp.float32)] * 4,
        compiler_params=pltpu.CompilerParams(
            dimension_semantics=("arbitrary", "arbitrary", "arbitrary"),
            vmem_limit_bytes=_vmem_limit(blk)),
        name="diff_attn",
    )(lq1, lk1, lq2, lk2, subw, qkv3, qkv3, qkv3, *cast)
    return outs[0], outs[1:]


def _neigh_bias(rpb, rows):
    heads = rpb.shape[0]
    rpb = rpb.astype(jnp.float32) * LOG2E
    qc = np.arange(GRID_W)[:, None]
    kc = np.arange(GRID_W)[None, :]
    col_start = np.clip(qc - NA_KW // 2, 0, GRID_W - NA_KW)
    col_ok = (kc >= col_start) & (kc < col_start + NA_KW)
    dc = kc - qc + (NA_KW - 1)
    tcol = jnp.full((heads, 2 * NA_KH - 1, GRID_W, GRID_W), NEG_BIG, jnp.float32)
    for v in range(2 * NA_KW - 1):
        tcol = jnp.where((col_ok & (dc == v))[None, None], rpb[:, :, v][:, :, None, None], tcol)
    strip = jnp.transpose(tcol, (0, 2, 1, 3)).reshape(heads, GRID_W, (2 * NA_KH - 1) * GRID_W)
    masked = lambda n: jnp.full((heads, GRID_W, n * GRID_W), NEG_BIG, jnp.float32)
    nblk = rows // NA_QROWS
    tables = []
    for rb, key_row0 in ((0, 0), (1, 0), (nblk - 1, (nblk - 3) * NA_QROWS)):
        lines = []
        for qr in range(NA_QROWS):
            r = rb * NA_QROWS + qr
            row_start = min(max(r - NA_KH // 2, 0), rows - NA_KH)
            kr_lo = max(row_start - key_row0, 0)
            kr_hi = min(row_start + NA_KH - 1 - key_row0, NA_KROWS - 1)
            dr_lo = key_row0 + kr_lo - r + NA_KH - 1
            pieces = [strip[:, :, dr_lo * GRID_W:(dr_lo + kr_hi - kr_lo + 1) * GRID_W]]
            if kr_lo > 0:
                pieces.insert(0, masked(kr_lo))
            if kr_hi < NA_KROWS - 1:
                pieces.append(masked(NA_KROWS - 1 - kr_hi))
            lines.append(jnp.concatenate(pieces, axis=-1))
        tables.append(jnp.concatenate(lines, axis=1))
    return jnp.stack(tables)


def _neigh_attn_kernel(bias_ref, q_ref, k_ref, v_ref, o_ref):
    d = NA_HEAD_DIM
    heads = q_ref.shape[2] // d

    def scores(h, token):
        cols = slice(h * d, (h + 1) * d)
        q = q_ref[0, :, cols]
        k = k_ref[0, :, cols]
        if token is not None:
            k = k + token
        s = lax.dot_general(q, k, _NT, preferred_element_type=jnp.float32) + bias_ref[0, h]
        return s, jnp.max(s, axis=1, keepdims=True)

    def consume(h, s, m):
        cols = slice(h * d, (h + 1) * d)
        p = jnp.exp2(s - m)
        l = jnp.sum(p, axis=1, keepdims=True)
        o = jnp.dot(p.astype(v_ref.dtype), v_ref[0, :, cols], preferred_element_type=jnp.float32)
        o_ref[0, :, cols] = (o / l).astype(o_ref.dtype)
        return _zero_after(o, k_ref.dtype)

    nxt = scores(0, None)
    token = None
    for h in range(heads):
        cur = nxt
        if h + 1 < heads:
            nxt = scores(h + 1, token)
        token = consume(h, *cur)


def _neigh_attn(qkv3, bias):
    b, seq, _ = qkv3.shape
    width = NA_HEADS * NA_HEAD_DIM
    nq = NA_QROWS * GRID_W
    nk = NA_KROWS * GRID_W
    nblk = seq // nq
    col0 = 3 * DA_HEADS * 2 * DA_HEAD_DIM
    start = lambda rb: jnp.clip(rb - 1, 0, nblk - 3) * nq
    block_class = lambda rb: jnp.where(rb == 0, 0, jnp.where(rb == nblk - 1, 2, 1))
    kv_spec = lambda base: pl.BlockSpec((pl.Element(1), pl.Element(nk), pl.Element(width)),
                                        lambda bi, rb: (bi, start(rb), col0 + base * width))
    blk = 2 * (NA_HEADS * nq * nk * 4 + 2 * nq * width * 2 + 2 * nk * width * 2) + 8 * nq * nk * 4
    return pl.pallas_call(
        _neigh_attn_kernel,
        out_shape=jax.ShapeDtypeStruct((b, seq, width), jnp.bfloat16),
        grid=(b, nblk),
        in_specs=[
            pl.BlockSpec((1, NA_HEADS, nq, nk), lambda bi, rb: (block_class(rb), 0, 0, 0)),
            pl.BlockSpec((1, nq, width), lambda bi, rb: (bi, rb, col0 // width)),
            kv_spec(1), kv_spec(2),
        ],
        out_specs=pl.BlockSpec((1, nq, width), lambda bi, rb: (bi, rb, 0)),
        compiler_params=pltpu.CompilerParams(
            dimension_semantics=("parallel", "arbitrary"),
            vmem_limit_bytes=_vmem_limit(blk)),
        name="neigh_attn",
    )(bias, qkv3, qkv3, qkv3)


def _merge_kernel(x_ref, h_ref, gpost_ref, gnext_ref, oa_ref, on_ref, wga_ref, wgb_ref, wa_ref, wb_ref, wo_ref,
                  o_ref, hn_ref, y_ref):
    j = pl.program_id(1)

    def accumulate(first):
        h = h_ref[...]
        f32 = jnp.float32
        ga = 1.0 / (1.0 + jnp.exp(-jnp.dot(h, wga_ref[...], preferred_element_type=f32)))
        gb = 1.0 / (1.0 + jnp.exp(-jnp.dot(h, wgb_ref[...], preferred_element_type=f32)))
        a = jnp.dot(oa_ref[...], wa_ref[...], preferred_element_type=f32)
        b = jnp.dot(on_ref[...], wb_ref[...], preferred_element_type=f32)
        mixed = (ga * a + gb * b).astype(wo_ref.dtype)
        tn = mixed.shape[1]
        wo = wo_ref[pl.ds(pl.multiple_of(j * tn, tn), tn), :]
        part = jnp.dot(mixed, wo, preferred_element_type=f32)
        y_ref[...] = part if first else y_ref[...] + part

    pl.when(j == 0)(lambda: accumulate(True))
    pl.when(j > 0)(lambda: accumulate(False))

    @pl.when(j == pl.num_programs(1) - 1)
    def _():
        x1 = x_ref[...] + _rms(y_ref[...], gpost_ref[...])
        o_ref[...] = x1
        hn_ref[...] = _rms(x1, gnext_ref[...]).astype(hn_ref.dtype)


def _merge(x2, h2, gpost, gnext, oa2, on2, w_in_bf16, wa, wb, wo, *, tm, tn):
    m, d = x2.shape
    ca = oa2.shape[1]
    cb = on2.shape[1]
    gate_col0 = (w_in_bf16.shape[1] - 2 * d) // tn
    nj = d // tn
    blk = (2 * (2 * tm * d * 4 + 2 * tm * d * 2 + tm * (ca + cb) * 2 + (2 * d + ca + cb) * tn * 2)
           + d * d * 2 + tm * d * 4 + 6 * tm * tn * 4)
    return pl.pallas_call(
        _merge_kernel,
        out_shape=(jax.ShapeDtypeStruct((m, d), jnp.float32), jax.ShapeDtypeStruct((m, d), jnp.bfloat16)),
        grid=(m // tm, nj),
        in_specs=[
            pl.BlockSpec((tm, d), lambda i, j: (i, 0)),
            pl.BlockSpec((tm, d), lambda i, j: (i, 0)),
            pl.BlockSpec((1, d), lambda i, j: (0, 0)),
            pl.BlockSpec((1, d), lambda i, j: (0, 0)),
            pl.BlockSpec((tm, ca), lambda i, j: (i, 0)),
            pl.BlockSpec((tm, cb), lambda i, j: (i, 0)),
            pl.BlockSpec((d, tn), lambda i, j: (0, gate_col0 + j)),
            pl.BlockSpec((d, tn), lambda i, j: (0, gate_col0 + nj + j)),
            pl.BlockSpec((ca, tn), lambda i, j: (0, j)),
            pl.BlockSpec((cb, tn), lambda i, j: (0, j)),
            pl.BlockSpec((d, d), lambda i, j: (0, 0), pipeline_mode=pl.Buffered(1)),
        ],
        out_specs=(pl.BlockSpec((tm, d), lambda i, j: (i, 0)), pl.BlockSpec((tm, d), lambda i, j: (i, 0))),
        scratch_shapes=[pltpu.VMEM((tm, d), jnp.float32)],
        compiler_params=pltpu.CompilerParams(
            dimension_semantics=("parallel", "arbitrary"),
            vmem_limit_bytes=_vmem_limit(blk)),
        name="merge",
    )(x2, h2, gpost, gnext, oa2, on2, w_in_bf16, w_in_bf16, wa, wb, wo)


def _mlp_kernel(x_ref, h_ref, gpost_ref, wu_ref, wd_ref, o_ref, acc_ref):
    j = pl.program_id(1)

    def accumulate(first):
        u = jnp.dot(h_ref[...], wu_ref[...], preferred_element_type=jnp.float32)
        u = jnp.square(jnp.maximum(u, 0.0)).astype(wd_ref.dtype)
        part = jnp.dot(u, wd_ref[...], preferred_element_type=jnp.float32)
        acc_ref[...] = part if first else acc_ref[...] + part

    pl.when(j == 0)(lambda: accumulate(True))
    pl.when(j > 0)(lambda: accumulate(False))

    @pl.when(j == pl.num_programs(1) - 1)
    def _():
        o_ref[...] = x_ref[...] + _rms(acc_ref[...], gpost_ref[...])


def _mlp(x2, h2, gpost, wu, wd, *, tm, tf):
    m, d = x2.shape
    f = wu.shape[1]
    blk = 2 * (2 * tm * d * 4 + tm * d * 2 + 2 * d * tf * 2) + tm * d * 4 + 2 * tm * tf * 4
    return pl.pallas_call(
        _mlp_kernel,
        out_shape=jax.ShapeDtypeStruct((m, d), jnp.float32),
        grid=(m // tm, f // tf),
        in_specs=[
            pl.BlockSpec((tm, d), lambda i, j: (i, 0)),
            pl.BlockSpec((tm, d), lambda i, j: (i, 0)),
            pl.BlockSpec((1, d), lambda i, j: (0, 0)),
            pl.BlockSpec((d, tf), lambda i, j: (0, j)),
            pl.BlockSpec((tf, d), lambda i, j: (j, 0)),
        ],
        out_specs=pl.BlockSpec((tm, d), lambda i, j: (i, 0)),
        scratch_shapes=[pltpu.VMEM((tm, d), jnp.float32)],
        compiler_params=pltpu.CompilerParams(
            dimension_semantics=("parallel", "arbitrary"),
            vmem_limit_bytes=_vmem_limit(blk)),
        name="mlp",
    )(x2, h2, gpost, wu, wd)


def _rope_tables(seq):
    inv = (1.0 / (np.float32(ROPE_THETA) ** (np.arange(0, DA_HEAD_DIM, 2, dtype=np.float32) / DA_HEAD_DIM)))
    ang = np.arange(seq, dtype=np.float32)[:, None] * inv.astype(np.float32)[None, :]
    cos, sin = np.cos(ang).astype(np.float32), np.sin(ang).astype(np.float32)
    return jnp.asarray(np.concatenate([cos, cos], axis=1)), jnp.asarray(np.concatenate([-sin, sin], axis=1))


def kernel(x, w_in, w_branch_a, w_branch_b, w_out, norm_mix_pre, norm_mix_post, norm_mlp_pre, norm_mlp_post,
           lam_q1, lam_k1, lam_q2, lam_k2, subln_w, na_rpb, w_up, w_down):
    b, seq, d = x.shape
    m = b * seq
    depth = w_in.shape[0]
    da_width = DA_HEADS * 2 * DA_HEAD_DIM
    na_width = NA_HEADS * NA_HEAD_DIM
    qkv_cols = 3 * da_width + 3 * na_width
    bf16 = jnp.bfloat16
    cosd, sind = _rope_tables(seq)
    row = lambda v: v.reshape(1, -1).astype(jnp.float32)
    tm = min(512, seq)
    tq = min(512, seq)
    n_sub = min(2, seq // tq)
    x2 = x.reshape(m, d)
    for l in range(depth):
        lambda_init = 0.8 - 0.6 * math.exp(-0.3 * l)
        w_in_l = w_in[l].astype(bf16)
        qkv, h_mix = _qkv_proj(x2, row(norm_mix_pre[l]), w_in_l, cosd, sind, seq,
                               tm=min(1024, seq), tn=2048, da_width=da_width, na_width=na_width)
        qkv3 = qkv.reshape(b, seq, qkv_cols)
        late = [w_branch_a[l], w_branch_b[l], w_out[l], w_up[l], w_down[l]]
        steps = b * DA_HEADS * (seq // (tq * n_sub))
        inside = [_cast_rows_per_step(w.shape[0], steps) is not None for w in late]
        oa, done = _diff_attn(qkv3, row(lam_q1[l]), row(lam_k1[l]), row(lam_q2[l]), row(lam_k2[l]), row(subln_w[l]),
                              [w for w, ok in zip(late, inside) if ok],
                              tq=tq, tk=min(1024, seq), n_sub=n_sub, lambda_init=lambda_init)
        done = list(done)
        wa, wb, wo, wu, wd = [done.pop(0) if ok else w.astype(bf16) for w, ok in zip(late, inside)]
        on = _neigh_attn(qkv3, _neigh_bias(na_rpb[l], seq // GRID_W))
        x2, h_mlp = _merge(x2, h_mix, row(norm_mix_post[l]), row(norm_mlp_pre[l]), oa.reshape(m, da_width),
                           on.reshape(m, na_width), w_in_l, wa, wb, wo, tm=tm, tn=512)
        x2 = _mlp(x2, h_mlp, row(norm_mlp_post[l]), wu, wd, tm=tm, tf=min(1024, w_up.shape[2]))
    return x2.reshape(b, seq, d)
```

```python
import functools
import math

import jax
import jax.numpy as jnp
import numpy as np
from jax import lax
from jax.experimental import pallas as pl
from jax.experimental.pallas import tpu as pltpu

EPS = 1e-6
ROPE_THETA = 10000.0
GRID_W = 64
DA_HEADS = 4
DA_HEAD_DIM = 128
NA_HEADS = 8
NA_HEAD_DIM = 128
NA_KH = 8
NA_KW = 16
LANES = 128
SUBLANES = 8
BF16_ROWS = 16
MXU_TILE = 256
NEG_BIG = -1e30
NA_QROWS = 4
NA_KROWS = 12
V7X_VMEM_BYTES = 64 * 1024 * 1024
assert DA_HEAD_DIM == NA_HEAD_DIM
LOG2E = math.log2(math.e)
QK_MULT = DA_HEAD_DIM ** -0.5 * LOG2E

_NT = (((1,), (1,)), ((), ()))


def _vmem_limit(block_bytes):
    return int(min(block_bytes + 16 * 1024 * 1024, V7X_VMEM_BYTES - 6 * 1024 * 1024))


def _rms(x, g):
    return x * lax.rsqrt(jnp.mean(x * x, axis=-1, keepdims=True) + EPS) * g


def _zero_after(x, dtype):
    bits = pltpu.bitcast(x[0:SUBLANES, 0:LANES], jnp.uint32)
    return pltpu.bitcast((bits >> 16) >> 16, jnp.float32)[0:1, :].astype(dtype)


def _qkv_kernel(x_ref, g_ref, w_ref, cos_ref, sin_ref, o_ref, h_ref, *, group_cols):
    j = pl.program_id(1)

    @pl.when(j == 0)
    def _():
        h_ref[...] = _rms(x_ref[...], g_ref[...]).astype(h_ref.dtype)

    acc = jnp.dot(h_ref[...], w_ref[...], preferred_element_type=jnp.float32)

    groups_per_tile = acc.shape[1] // group_cols
    for gi in range(groups_per_tile):
        group = j * groups_per_tile + gi
        rot = jnp.where(group == 0, QK_MULT, jnp.where(group == 1, 1.0, 0.0)).astype(jnp.float32)
        flat = jnp.where(group == 3, QK_MULT, jnp.where(group <= 1, 0.0, 1.0)).astype(jnp.float32)
        cs = cos_ref[...] * rot + flat
        sn = sin_ref[...] * rot
        for g in range(gi * group_cols // LANES, (gi + 1) * group_cols // LANES):
            y = acc[:, g * LANES:(g + 1) * LANES]
            o_ref[:, g * LANES:(g + 1) * LANES] = (y * cs + pltpu.roll(y, LANES // 2, 1) * sn).astype(o_ref.dtype)


def _qkv_proj(x2, gain, w_bf16, cosd, sind, seq, *, tm, tn, da_width, na_width):
    m, d = x2.shape
    assert da_width == na_width and tn % da_width == 0
    n_cols = 3 * da_width + 3 * na_width
    pos_blocks = seq // tm
    blk = 2 * (tm * d * 4 + d * tn * 2 + tm * tn * 2 + 2 * tm * LANES * 4 + tm * d * 2) + tm * tn * 4
    return pl.pallas_call(
        functools.partial(_qkv_kernel, group_cols=da_width),
        out_shape=(jax.ShapeDtypeStruct((m, n_cols), jnp.bfloat16), jax.ShapeDtypeStruct((m, d), jnp.bfloat16)),
        grid=(m // tm, n_cols // tn),
        in_specs=[
            pl.BlockSpec((tm, d), lambda i, j: (i, 0)),
            pl.BlockSpec((1, d), lambda i, j: (0, 0)),
            pl.BlockSpec((d, tn), lambda i, j: (0, j)),
            pl.BlockSpec((tm, LANES), lambda i, j: (i % pos_blocks, 0)),
            pl.BlockSpec((tm, LANES), lambda i, j: (i % pos_blocks, 0)),
        ],
        out_specs=(pl.BlockSpec((tm, tn), lambda i, j: (i, j)), pl.BlockSpec((tm, d), lambda i, j: (i, 0))),
        compiler_params=pltpu.CompilerParams(
            dimension_semantics=("parallel", "arbitrary"),
            vmem_limit_bytes=_vmem_limit(blk)),
        name="qkv_proj",
    )(x2, gain, w_bf16, cosd, sind)


def _diff_attn_kernel(*refs, tq, tk, lambda_init, n_cast):
    lq1_ref, lk1_ref, lq2_ref, lk2_ref, subw_ref, q_ref, k_ref, v_ref = refs[:8]
    cast_in = refs[8:8 + n_cast]
    o_ref = refs[8 + n_cast]
    cast_out = refs[9 + n_cast:9 + 2 * n_cast]
    m_ref, l_ref, a_ref, sa1, sa2, sb1, sb2, ma1, ma2, mb1, mb2 = refs[9 + 2 * n_cast:]
    for w_ref, wo_ref in zip(cast_in, cast_out):
        wo_ref[...] = w_ref[...].astype(wo_ref.dtype)
    seq = k_ref.shape[1]
    n_chunks = seq // tk
    n_sub = q_ref.shape[1] // tq
    d = DA_HEAD_DIM
    lam = (jnp.exp(jnp.sum(lq1_ref[...] * lk1_ref[...])) - jnp.exp(jnp.sum(lq2_ref[...] * lk2_ref[...]))
           + lambda_init)

    def scores(item, dst, token):
        sub, c = item
        for half in range(2):
            q = q_ref[0, sub * tq:(sub + 1) * tq, half * d:(half + 1) * d]
            mx = None
            for t in range(tk // MXU_TILE):
                k = k_ref[0, pl.ds(c * tk + t * MXU_TILE, MXU_TILE), half * d:(half + 1) * d]
                if token is not None:
                    k = k + token
                s = lax.dot_general(q, k, _NT, preferred_element_type=jnp.float32)
                dst[half][:, t * MXU_TILE:(t + 1) * MXU_TILE] = s
                for u in range(MXU_TILE // LANES):
                    su = s[:, u * LANES:(u + 1) * LANES]
                    mx = su if mx is None else jnp.maximum(mx, su)
            dst[2 + half][...] = jnp.broadcast_to(jnp.max(mx, axis=1, keepdims=True), mx.shape)

    def consume(item, src):
        sub, c = item
        token = None
        for half in range(2):
            if c == 0:
                m_prev = jnp.full((tq, LANES), NEG_BIG, jnp.float32)
            else:
                m_prev = m_ref[sub, half]
            m_new = jnp.maximum(m_prev, src[2 + half][...])
            alpha = jnp.exp2(m_prev - m_new)
            m_rep = jnp.concatenate([m_new] * (MXU_TILE // LANES), axis=1)
            psum = None
            pv = None
            for t in range(tk // MXU_TILE):
                p = jnp.exp2(src[half][:, t * MXU_TILE:(t + 1) * MXU_TILE] - m_rep)
                for u in range(MXU_TILE // LANES):
                    pu = p[:, u * LANES:(u + 1) * LANES]
                    psum = pu if psum is None else psum + pu
                v = v_ref[0, pl.ds(c * tk + t * MXU_TILE, MXU_TILE), :]
                pvt = jnp.dot(p.astype(v.dtype), v, preferred_element_type=jnp.float32)
                pv = pvt if pv is None else pv + pvt
            if c == 0:
                l_ref[sub, half] = psum
                a_ref[sub, half] = pv
            else:
                l_ref[sub, half] = alpha * l_ref[sub, half] + psum
                a_ref[sub, half] = a_ref[sub, half] * jnp.concatenate([alpha] * (2 * d // LANES), axis=1) + pv
            m_ref[sub, half] = m_new
            if half == 0:
                token = _zero_after(pv, k_ref.dtype)
        return token

    def finalize(sub):
        o1 = a_ref[sub, 0] / jnp.sum(l_ref[sub, 0], axis=1, keepdims=True)
        o2 = a_ref[sub, 1] / jnp.sum(l_ref[sub, 1], axis=1, keepdims=True)
        o = _rms(o1 - lam * o2, subw_ref[...]) * (1.0 - lambda_init)
        o_ref[0, sub * tq:(sub + 1) * tq, :] = o.astype(o_ref.dtype)

    bufs = ((sa1, sa2, ma1, ma2), (sb1, sb2, mb1, mb2))
    items = [(sub, c) for sub in range(n_sub) for c in range(n_chunks)]
    scores(items[0], bufs[0], None)
    token = None
    for n, item in enumerate(items):
        if n + 1 < len(items):
            scores(items[n + 1], bufs[(n + 1) % 2], token)
        token = consume(item, bufs[n % 2])
        if item[1] == n_chunks - 1:
            finalize(item[0])


def _cast_rows_per_step(rows, steps):
    r = max(BF16_ROWS, rows // steps)
    return r if rows % r == 0 and rows // r <= steps else None


def _diff_attn(qkv3, lq1, lk1, lq2, lk2, subw, cast, *, tq, tk, n_sub, lambda_init):
    b, seq, _ = qkv3.shape
    hd = 2 * DA_HEAD_DIM
    tb = n_sub * tq
    nq = seq // tb
    steps = b * DA_HEADS * nq
    kern = functools.partial(_diff_attn_kernel, tq=tq, tk=tk, lambda_init=lambda_init, n_cast=len(cast))
    vec = lambda n: pl.BlockSpec((1, n), lambda bi, h, qi: (0, 0))

    def cast_spec(w):
        r = _cast_rows_per_step(w.shape[0], steps)
        nb = w.shape[0] // r
        return pl.BlockSpec((r, w.shape[1]), lambda bi, h, qi: (((bi * DA_HEADS + h) * nq + qi) * nb // steps, 0))

    cast_specs = [cast_spec(w) for w in cast]
    blk = (2 * (2 * seq * hd * 2 + 2 * tb * hd * 2) + 2 * n_sub * (2 * tq * LANES * 4 + tq * hd * 4)
           + 6 * tq * tk * 4 + sum(2 * sp.block_shape[0] * sp.block_shape[1] * 6 for sp in cast_specs))
    outs = pl.pallas_call(
        kern,
        out_shape=[jax.ShapeDtypeStruct((b, seq, DA_HEADS * hd), jnp.bfloat16)]
        + [jax.ShapeDtypeStruct(w.shape, jnp.bfloat16) for w in cast],
        grid=(b, DA_HEADS, nq),
        in_specs=[
            vec(DA_HEAD_DIM), vec(DA_HEAD_DIM), vec(DA_HEAD_DIM), vec(DA_HEAD_DIM), vec(hd),
            pl.BlockSpec((1, tb, hd), lambda bi, h, qi: (bi, qi, h)),
            pl.BlockSpec((1, seq, hd), lambda bi, h, qi: (bi, 0, DA_HEADS + h)),
            pl.BlockSpec((1, seq, hd), lambda bi, h, qi: (bi, 0, 2 * DA_HEADS + h)),
        ] + cast_specs,
        out_specs=[pl.BlockSpec((1, tb, hd), lambda bi, h, qi: (bi, qi, h))] + cast_specs,
        scratch_shapes=[
            pltpu.VMEM((n_sub, 2, tq, LANES), jnp.float32),
            pltpu.VMEM((n_sub, 2, tq, LANES), jnp.float32),
            pltpu.VMEM((n_sub, 2, tq, hd), jnp.float32),
        ] + [pltpu.VMEM((tq, tk), jnp.float32)] * 4 + [pltpu.VMEM((tq, LANES), jnp.float32)] * 4,
        compiler_params=pltpu.CompilerParams(
            dimension_semantics=("arbitrary", "arbitrary", "arbitrary"),
            vmem_limit_bytes=_vmem_limit(blk)),
        name="diff_attn",
    )(lq1, lk1, lq2, lk2, subw, qkv3, qkv3, qkv3, *cast)
    return outs[0], outs[1:]


def _neigh_bias(rpb, rows):
    heads = rpb.shape[0]
    rpb = rpb.astype(jnp.float32) * LOG2E
    qc = np.arange(GRID_W)[:, None]
    kc = np.arange(GRID_W)[None, :]
    col_start = np.clip(qc - NA_KW // 2, 0, GRID_W - NA_KW)
    col_ok = (kc >= col_start) & (kc < col_start + NA_KW)
    dc = kc - qc + (NA_KW - 1)
    tcol = jnp.full((heads, 2 * NA_KH - 1, GRID_W, GRID_W), NEG_BIG, jnp.float32)
    for v in range(2 * NA_KW - 1):
        tcol = jnp.where((col_ok & (dc == v))[None, None], rpb[:, :, v][:, :, None, None], tcol)
    strip = jnp.transpose(tcol, (0, 2, 1, 3)).reshape(heads, GRID_W, (2 * NA_KH - 1) * GRID_W)
    masked = lambda n: jnp.full((heads, GRID_W, n * GRID_W), NEG_BIG, jnp.float32)
    nblk = rows // NA_QROWS
    tables = []
    for rb, key_row0 in ((0, 0), (1, 0), (nblk - 1, (nblk - 3) * NA_QROWS)):
        lines = []
        for qr in range(NA_QROWS):
            r = rb * NA_QROWS + qr
            row_start = min(max(r - NA_KH // 2, 0), rows - NA_KH)
            kr_lo = max(row_start - key_row0, 0)
            kr_hi = min(row_start + NA_KH - 1 - key_row0, NA_KROWS - 1)
            dr_lo = key_row0 + kr_lo - r + NA_KH - 1
            pieces = [strip[:, :, dr_lo * GRID_W:(dr_lo + kr_hi - kr_lo + 1) * GRID_W]]
            if kr_lo > 0:
                pieces.insert(0, masked(kr_lo))
            if kr_hi < NA_KROWS - 1:
                pieces.append(masked(NA_KROWS - 1 - kr_hi))
            lines.append(jnp.concatenate(pieces, axis=-1))
        tables.append(jnp.concatenate(lines, axis=1))
    return jnp.stack(tables)


def _neigh_attn_kernel(bias0_ref, bias1_ref, q_ref, k_ref, v_ref, o_ref, *, nq, nk):
    d = NA_HEAD_DIM
    heads = q_ref.shape[2] // d
    step, last = pl.program_id(1), pl.num_programs(1) - 1
    offs = (jnp.where(step == last, nq, 0), jnp.where(step == 0, 0, nq))
    biases = (bias0_ref, bias1_ref)

    def scores(item, token):
        g, h = item
        cols = slice(h * d, (h + 1) * d)
        q = q_ref[0, g * nq:(g + 1) * nq, cols]
        k = k_ref[0, pl.ds(pl.multiple_of(offs[g], nq), nk), cols]
        if token is not None:
            k = k + token
        s = lax.dot_general(q, k, _NT, preferred_element_type=jnp.float32) + biases[g][0, h]
        return s, jnp.max(s, axis=1, keepdims=True)

    def consume(item, s, m):
        g, h = item
        cols = slice(h * d, (h + 1) * d)
        p = jnp.exp2(s - m)
        l = jnp.sum(p, axis=1, keepdims=True)
        v = v_ref[0, pl.ds(pl.multiple_of(offs[g], nq), nk), cols]
        o = jnp.dot(p.astype(v.dtype), v, preferred_element_type=jnp.float32)
        o_ref[0, g * nq:(g + 1) * nq, cols] = (o / l).astype(o_ref.dtype)
        return _zero_after(o, k_ref.dtype)

    items = [(g, h) for g in range(2) for h in range(heads)]
    nxt = scores(items[0], None)
    token = None
    for n, item in enumerate(items):
        cur = nxt
        if n + 1 < len(items):
            nxt = scores(items[n + 1], token)
        token = consume(item, *cur)


def _neigh_attn(qkv3, bias):
    b, seq, _ = qkv3.shape
    width = NA_HEADS * NA_HEAD_DIM
    nq = NA_QROWS * GRID_W
    nk = NA_KROWS * GRID_W
    nblk = seq // nq
    assert nblk % 2 == 0 and nblk >= 4
    col0 = 3 * DA_HEADS * 2 * DA_HEAD_DIM
    first_key = lambda s: jnp.clip(2 * s - 1, 0, nblk - 4) * nq
    class0 = lambda s: jnp.where(s == 0, 0, 1)
    class1 = lambda s: jnp.where(s == nblk // 2 - 1, 2, 1)
    kv_spec = lambda base: pl.BlockSpec((pl.Element(1), pl.Element(nk + nq), pl.Element(width)),
                                        lambda bi, s: (bi, first_key(s), col0 + base * width))
    bias_spec = lambda cls: pl.BlockSpec((1, NA_HEADS, nq, nk), lambda bi, s: (cls(s), 0, 0, 0))
    blk = 2 * (2 * NA_HEADS * nq * nk * 4 + 4 * nq * width * 2 + 2 * (nk + nq) * width * 2) + 8 * nq * nk * 4
    return pl.pallas_call(
        functools.partial(_neigh_attn_kernel, nq=nq, nk=nk),
        out_shape=jax.ShapeDtypeStruct((b, seq, width), jnp.bfloat16),
        grid=(b, nblk // 2),
        in_specs=[
            bias_spec(class0), bias_spec(class1),
            pl.BlockSpec((1, 2 * nq, width), lambda bi, s: (bi, s, col0 // width)),
            kv_spec(1), kv_spec(2),
        ],
        out_specs=pl.BlockSpec((1, 2 * nq, width), lambda bi, s: (bi, s, 0)),
        compiler_params=pltpu.CompilerParams(
            dimension_semantics=("parallel", "arbitrary"),
            vmem_limit_bytes=_vmem_limit(blk)),
        name="neigh_attn",
    )(bias, bias, qkv3, qkv3, qkv3)


def _merge_kernel(x_ref, h_ref, gpost_ref, gnext_ref, oa_ref, on_ref, wga_ref, wgb_ref, wa_ref, wb_ref, wo_ref,
                  o_ref, hn_ref, y_ref):
    j = pl.program_id(1)

    def accumulate(first):
        h = h_ref[...]
        f32 = jnp.float32
        ga = 1.0 / (1.0 + jnp.exp(-jnp.dot(h, wga_ref[...], preferred_element_type=f32)))
        gb = 1.0 / (1.0 + jnp.exp(-jnp.dot(h, wgb_ref[...], preferred_element_type=f32)))
        a = jnp.dot(oa_ref[...], wa_ref[...], preferred_element_type=f32)
        b = jnp.dot(on_ref[...], wb_ref[...], preferred_element_type=f32)
        mixed = (ga * a + gb * b).astype(wo_ref.dtype)
        tn = mixed.shape[1]
        wo = wo_ref[pl.ds(pl.multiple_of(j * tn, tn), tn), :]
        part = jnp.dot(mixed, wo, preferred_element_type=f32)
        y_ref[...] = part if first else y_ref[...] + part

    pl.when(j == 0)(lambda: accumulate(True))
    pl.when(j > 0)(lambda: accumulate(False))

    @pl.when(j == pl.num_programs(1) - 1)
    def _():
        x1 = x_ref[...] + _rms(y_ref[...], gpost_ref[...])
        o_ref[...] = x1
        hn_ref[...] = _rms(x1, gnext_ref[...]).astype(hn_ref.dtype)


def _merge(x2, h2, gpost, gnext, oa2, on2, w_in_bf16, wa, wb, wo, *, tm, tn):
    m, d = x2.shape
    ca = oa2.shape[1]
    cb = on2.shape[1]
    gate_col0 = (w_in_bf16.shape[1] - 2 * d) // tn
    nj = d // tn
    blk = (2 * (2 * tm * d * 4 + 2 * tm * d * 2 + tm * (ca + cb) * 2 + (2 * d + ca + cb) * tn * 2)
           + d * d * 2 + tm * d * 4 + 6 * tm * tn * 4)
    return pl.pallas_call(
        _merge_kernel,
        out_shape=(jax.ShapeDtypeStruct((m, d), jnp.float32), jax.ShapeDtypeStruct((m, d), jnp.bfloat16)),
        grid=(m // tm, nj),
        in_specs=[
            pl.BlockSpec((tm, d), lambda i, j: (i, 0)),
            pl.BlockSpec((tm, d), lambda i, j: (i, 0)),
            pl.BlockSpec((1, d), lambda i, j: (0, 0)),
            pl.BlockSpec((1, d), lambda i, j: (0, 0)),
            pl.BlockSpec((tm, ca), lambda i, j: (i, 0)),
            pl.BlockSpec((tm, cb), lambda i, j: (i, 0)),
            pl.BlockSpec((d, tn), lambda i, j: (0, gate_col0 + j)),
            pl.BlockSpec((d, tn), lambda i, j: (0, gate_col0 + nj + j)),
            pl.BlockSpec((ca, tn), lambda i, j: (0, j)),
            pl.BlockSpec((cb, tn), lambda i, j: (0, j)),
            pl.BlockSpec((d, d), lambda i, j: (0, 0), pipeline_mode=pl.Buffered(1)),
        ],
        out_specs=(pl.BlockSpec((tm, d), lambda i, j: (i, 0)), pl.BlockSpec((tm, d), lambda i, j: (i, 0))),
        scratch_shapes=[pltpu.VMEM((tm, d), jnp.float32)],
        compiler_params=pltpu.CompilerParams(
            dimension_semantics=("parallel", "arbitrary"),
            vmem_limit_bytes=_vmem_limit(blk)),
        name="merge",
    )(x2, h2, gpost, gnext, oa2, on2, w_in_bf16, w_in_bf16, wa, wb, wo)


def _mlp_kernel(x_ref, h_ref, gpost_ref, wu_ref, wd_ref, o_ref, acc_ref):
    j = pl.program_id(1)

    def accumulate(first):
        u = jnp.dot(h_ref[...], wu_ref[...], preferred_element_type=jnp.float32)
        u = jnp.square(jnp.maximum(u, 0.0)).astype(wd_ref.dtype)
        part = jnp.dot(u, wd_ref[...], preferred_element_type=jnp.float32)
        acc_ref[...] = part if first else acc_ref[...] + part

    pl.when(j == 0)(lambda: accumulate(True))
    pl.when(j > 0)(lambda: accumulate(False))

    @pl.when(j == pl.num_programs(1) - 1)
    def _():
        o_ref[...] = x_ref[...] + _rms(acc_ref[...], gpost_ref[...])


def _mlp(x2, h2, gpost, wu, wd, *, tm, tf):
    m, d = x2.shape
    f = wu.shape[1]
    blk = 2 * (2 * tm * d * 4 + tm * d * 2 + 2 * d * tf * 2) + tm * d * 4 + 2 * tm * tf * 4
    return pl.pallas_call(
        _mlp_kernel,
        out_shape=jax.ShapeDtypeStruct((m, d), jnp.float32),
        grid=(m // tm, f // tf),
        in_specs=[
            pl.BlockSpec((tm, d), lambda i, j: (i, 0)),
            pl.BlockSpec((tm, d), lambda i, j: (i, 0)),
            pl.BlockSpec((1, d), lambda i, j: (0, 0)),
            pl.BlockSpec((d, tf), lambda i, j: (0, j)),
            pl.BlockSpec((tf, d), lambda i, j: (j, 0)),
        ],
        out_specs=pl.BlockSpec((tm, d), lambda i, j: (i, 0)),
        scratch_shapes=[pltpu.VMEM((tm, d), jnp.float32)],
        compiler_params=pltpu.CompilerParams(
            dimension_semantics=("parallel", "arbitrary"),
            vmem_limit_bytes=_vmem_limit(blk)),
        name="mlp",
    )(x2, h2, gpost, wu, wd)


def _rope_tables(seq):
    inv = (1.0 / (np.float32(ROPE_THETA) ** (np.arange(0, DA_HEAD_DIM, 2, dtype=np.float32) / DA_HEAD_DIM)))
    ang = np.arange(seq, dtype=np.float32)[:, None] * inv.astype(np.float32)[None, :]
    cos, sin = np.cos(ang).astype(np.float32), np.sin(ang).astype(np.float32)
    return jnp.asarray(np.concatenate([cos, cos], axis=1)), jnp.asarray(np.concatenate([-sin, sin], axis=1))


def kernel(x, w_in, w_branch_a, w_branch_b, w_out, norm_mix_pre, norm_mix_post, norm_mlp_pre, norm_mlp_post,
           lam_q1, lam_k1, lam_q2, lam_k2, subln_w, na_rpb, w_up, w_down):
    b, seq, d = x.shape
    m = b * seq
    depth = w_in.shape[0]
    da_width = DA_HEADS * 2 * DA_HEAD_DIM
    na_width = NA_HEADS * NA_HEAD_DIM
    qkv_cols = 3 * da_width + 3 * na_width
    bf16 = jnp.bfloat16
    cosd, sind = _rope_tables(seq)
    row = lambda v: v.reshape(1, -1).astype(jnp.float32)
    tm = min(512, seq)
    tq = min(512, seq)
    n_sub = min(2, seq // tq)
    x2 = x.reshape(m, d)
    for l in range(depth):
        lambda_init = 0.8 - 0.6 * math.exp(-0.3 * l)
        w_in_l = w_in[l].astype(bf16)
        qkv, h_mix = _qkv_proj(x2, row(norm_mix_pre[l]), w_in_l, cosd, sind, seq,
                               tm=min(1024, seq), tn=2048, da_width=da_width, na_width=na_width)
        qkv3 = qkv.reshape(b, seq, qkv_cols)
        late = [w_branch_a[l], w_branch_b[l], w_out[l], w_up[l], w_down[l]]
        steps = b * DA_HEADS * (seq // (tq * n_sub))
        inside = [_cast_rows_per_step(w.shape[0], steps) is not None for w in late]
        oa, done = _diff_attn(qkv3, row(lam_q1[l]), row(lam_k1[l]), row(lam_q2[l]), row(lam_k2[l]), row(subln_w[l]),
                              [w for w, ok in zip(late, inside) if ok],
                              tq=tq, tk=min(1024, seq), n_sub=n_sub, lambda_init=lambda_init)
        done = list(done)
        wa, wb, wo, wu, wd = [done.pop(0) if ok else w.astype(bf16) for w, ok in zip(late, inside)]
        on = _neigh_attn(qkv3, _neigh_bias(na_rpb[l], seq // GRID_W))
        x2, h_mlp = _merge(x2, h_mix, row(norm_mix_post[l]), row(norm_mlp_pre[l]), oa.reshape(m, da_width),
                           on.reshape(m, na_width), w_in_l, wa, wb, wo, tm=tm, tn=512)
        x2 = _mlp(x2, h_mlp, row(norm_mlp_post[l]), wu, wd, tm=tm, tf=min(1024, w_up.shape[2]))
    return x2.reshape(b, seq, d)
```

```python
import functools
import math

import jax
import jax.numpy as jnp
import numpy as np
from jax import lax
from jax.experimental import pallas as pl
from jax.experimental.pallas import tpu as pltpu

EPS = 1e-6
ROPE_THETA = 10000.0
GRID_W = 64
DA_HEADS = 4
DA_HEAD_DIM = 128
NA_HEADS = 8
NA_HEAD_DIM = 128
NA_KH = 8
NA_KW = 16
LANES = 128
SUBLANES = 8
BF16_ROWS = 16
MXU_TILE = 256
NEG_BIG = -1e30
NA_QROWS = 4
NA_KROWS = 12
V7X_VMEM_BYTES = 64 * 1024 * 1024
assert DA_HEAD_DIM == NA_HEAD_DIM
LOG2E = math.log2(math.e)
QK_MULT = DA_HEAD_DIM ** -0.5 * LOG2E

_NT = (((1,), (1,)), ((), ()))


def _vmem_limit(block_bytes):
    return int(min(block_bytes + 16 * 1024 * 1024, V7X_VMEM_BYTES - 6 * 1024 * 1024))


def _rms(x, g):
    return x * lax.rsqrt(jnp.mean(x * x, axis=-1, keepdims=True) + EPS) * g


def _zero_after(x, dtype):
    bits = pltpu.bitcast(x[0:SUBLANES, 0:LANES], jnp.uint32)
    return pltpu.bitcast((bits >> 16) >> 16, jnp.float32)[0:1, :].astype(dtype)


def _qkv_kernel(x_ref, g_ref, w_ref, cos_ref, sin_ref, o_ref, h_ref, *, group_cols):
    j = pl.program_id(1)

    @pl.when(j == 0)
    def _():
        h_ref[...] = _rms(x_ref[...], g_ref[...]).astype(h_ref.dtype)

    acc = jnp.dot(h_ref[...], w_ref[...], preferred_element_type=jnp.float32)

    groups_per_tile = acc.shape[1] // group_cols
    for gi in range(groups_per_tile):
        group = j * groups_per_tile + gi
        rot = jnp.where(group == 0, QK_MULT, jnp.where(group == 1, 1.0, 0.0)).astype(jnp.float32)
        flat = jnp.where(group == 3, QK_MULT, jnp.where(group <= 1, 0.0, 1.0)).astype(jnp.float32)
        cs = cos_ref[...] * rot + flat
        sn = sin_ref[...] * rot
        for g in range(gi * group_cols // LANES, (gi + 1) * group_cols // LANES):
            y = acc[:, g * LANES:(g + 1) * LANES]
            o_ref[:, g * LANES:(g + 1) * LANES] = (y * cs + pltpu.roll(y, LANES // 2, 1) * sn).astype(o_ref.dtype)


def _qkv_proj(x2, gain, w_bf16, cosd, sind, seq, *, tm, tn, da_width, na_width):
    m, d = x2.shape
    assert da_width == na_width and tn % da_width == 0
    n_cols = 3 * da_width + 3 * na_width
    pos_blocks = seq // tm
    blk = 2 * (tm * d * 4 + d * tn * 2 + tm * tn * 2 + 2 * tm * LANES * 4 + tm * d * 2) + tm * tn * 4
    return pl.pallas_call(
        functools.partial(_qkv_kernel, group_cols=da_width),
        out_shape=(jax.ShapeDtypeStruct((m, n_cols), jnp.bfloat16), jax.ShapeDtypeStruct((m, d), jnp.bfloat16)),
        grid=(m // tm, n_cols // tn),
        in_specs=[
            pl.BlockSpec((tm, d), lambda i, j: (i, 0)),
            pl.BlockSpec((1, d), lambda i, j: (0, 0)),
            pl.BlockSpec((d, tn), lambda i, j: (0, j)),
            pl.BlockSpec((tm, LANES), lambda i, j: (i % pos_blocks, 0)),
            pl.BlockSpec((tm, LANES), lambda i, j: (i % pos_blocks, 0)),
        ],
        out_specs=(pl.BlockSpec((tm, tn), lambda i, j: (i, j)), pl.BlockSpec((tm, d), lambda i, j: (i, 0))),
        compiler_params=pltpu.CompilerParams(
            dimension_semantics=("parallel", "arbitrary"),
            vmem_limit_bytes=_vmem_limit(blk)),
        name="qkv_proj",
    )(x2, gain, w_bf16, cosd, sind)


def _diff_attn_kernel(*refs, tq, tk, lambda_init, n_cast):
    lq1_ref, lk1_ref, lq2_ref, lk2_ref, subw_ref, q_ref, k_ref, v_ref = refs[:8]
    cast_in = refs[8:8 + n_cast]
    o_ref = refs[8 + n_cast]
    cast_out = refs[9 + n_cast:9 + 2 * n_cast]
    m_ref, l_ref, a_ref, sa1, sa2, sb1, sb2, ma1, ma2, mb1, mb2 = refs[9 + 2 * n_cast:]
    for w_ref, wo_ref in zip(cast_in, cast_out):
        wo_ref[...] = w_ref[...].astype(wo_ref.dtype)
    seq = k_ref.shape[1]
    n_chunks = seq // tk
    n_sub = q_ref.shape[1] // tq
    d = DA_HEAD_DIM
    lam = (jnp.exp(jnp.sum(lq1_ref[...] * lk1_ref[...])) - jnp.exp(jnp.sum(lq2_ref[...] * lk2_ref[...]))
           + lambda_init)

    def scores(item, dst, token):
        sub, c = item
        for half in range(2):
            q = q_ref[0, sub * tq:(sub + 1) * tq, half * d:(half + 1) * d]
            mx = None
            for t in range(tk // MXU_TILE):
                k = k_ref[0, pl.ds(c * tk + t * MXU_TILE, MXU_TILE), half * d:(half + 1) * d]
                if token is not None:
                    k = k + token
                s = lax.dot_general(q, k, _NT, preferred_element_type=jnp.float32)
                dst[half][:, t * MXU_TILE:(t + 1) * MXU_TILE] = s
                for u in range(MXU_TILE // LANES):
                    su = s[:, u * LANES:(u + 1) * LANES]
                    mx = su if mx is None else jnp.maximum(mx, su)
            dst[2 + half][...] = jnp.broadcast_to(jnp.max(mx, axis=1, keepdims=True), mx.shape)

    def consume(item, src):
        sub, c = item
        token = None
        for half in range(2):
            if c == 0:
                m_prev = jnp.full((tq, LANES), NEG_BIG, jnp.float32)
            else:
                m_prev = m_ref[sub, half]
            m_new = jnp.maximum(m_prev, src[2 + half][...])
            alpha = jnp.exp2(m_prev - m_new)
            m_rep = jnp.concatenate([m_new] * (MXU_TILE // LANES), axis=1)
            psum = None
            pv = None
            for t in range(tk // MXU_TILE):
                p = jnp.exp2(src[half][:, t * MXU_TILE:(t + 1) * MXU_TILE] - m_rep)
                for u in range(MXU_TILE // LANES):
                    pu = p[:, u * LANES:(u + 1) * LANES]
                    psum = pu if psum is None else psum + pu
                v = v_ref[0, pl.ds(c * tk + t * MXU_TILE, MXU_TILE), :]
                pvt = jnp.dot(p.astype(v.dtype), v, preferred_element_type=jnp.float32)
                pv = pvt if pv is None else pv + pvt
            if c == 0:
                l_ref[sub, half] = psum
                a_ref[sub, half] = pv
            else:
                l_ref[sub, half] = alpha * l_ref[sub, half] + psum
                a_ref[sub, half] = a_ref[sub, half] * jnp.concatenate([alpha] * (2 * d // LANES), axis=1) + pv
            m_ref[sub, half] = m_new
            if half == 0:
                token = _zero_after(pv, k_ref.dtype)
        return token

    def finalize(sub):
        o1 = a_ref[sub, 0] / jnp.sum(l_ref[sub, 0], axis=1, keepdims=True)
        o2 = a_ref[sub, 1] / jnp.sum(l_ref[sub, 1], axis=1, keepdims=True)
        o = _rms(o1 - lam * o2, subw_ref[...]) * (1.0 - lambda_init)
        o_ref[0, sub * tq:(sub + 1) * tq, :] = o.astype(o_ref.dtype)

    bufs = ((sa1, sa2, ma1, ma2), (sb1, sb2, mb1, mb2))
    items = [(sub, c) for sub in range(n_sub) for c in range(n_chunks)]
    scores(items[0], bufs[0], None)
    token = None
    for n, item in enumerate(items):
        if n + 1 < len(items):
            scores(items[n + 1], bufs[(n + 1) % 2], token)
        token = consume(item, bufs[n % 2])
        if item[1] == n_chunks - 1:
            finalize(item[0])


def _cast_rows_per_step(rows, steps):
    r = max(BF16_ROWS, rows // steps)
    return r if rows % r == 0 and rows // r <= steps else None


def _diff_attn(qkv3, lq1, lk1, lq2, lk2, subw, cast, *, tq, tk, n_sub, lambda_init):
    b, seq, _ = qkv3.shape
    hd = 2 * DA_HEAD_DIM
    tb = n_sub * tq
    nq = seq // tb
    steps = b * DA_HEADS * nq
    kern = functools.partial(_diff_attn_kernel, tq=tq, tk=tk, lambda_init=lambda_init, n_cast=len(cast))
    vec = lambda n: pl.BlockSpec((1, n), lambda bi, h, qi: (0, 0))

    def cast_spec(w):
        r = _cast_rows_per_step(w.shape[0], steps)
        nb = w.shape[0] // r
        return pl.BlockSpec((r, w.shape[1]), lambda bi, h, qi: (((bi * DA_HEADS + h) * nq + qi) * nb // steps, 0))

    cast_specs = [cast_spec(w) for w in cast]
    blk = (2 * (2 * seq * hd * 2 + 2 * tb * hd * 2) + 2 * n_sub * (2 * tq * LANES * 4 + tq * hd * 4)
           + 6 * tq * tk * 4 + sum(2 * sp.block_shape[0] * sp.block_shape[1] * 6 for sp in cast_specs))
    outs = pl.pallas_call(
        kern,
        out_shape=[jax.ShapeDtypeStruct((b, seq, DA_HEADS * hd), jnp.bfloat16)]
        + [jax.ShapeDtypeStruct(w.shape, jnp.bfloat16) for w in cast],
        grid=(b, DA_HEADS, nq),
        in_specs=[
            vec(DA_HEAD_DIM), vec(DA_HEAD_DIM), vec(DA_HEAD_DIM), vec(DA_HEAD_DIM), vec(hd),
            pl.BlockSpec((1, tb, hd), lambda bi, h, qi: (bi, qi, h)),
            pl.BlockSpec((1, seq, hd), lambda bi, h, qi: (bi, 0, DA_HEADS + h)),
            pl.BlockSpec((1, seq, hd), lambda bi, h, qi: (bi, 0, 2 * DA_HEADS + h)),
        ] + cast_specs,
        out_specs=[pl.BlockSpec((1, tb, hd), lambda bi, h, qi: (bi, qi, h))] + cast_specs,
        scratch_shapes=[
            pltpu.VMEM((n_sub, 2, tq, LANES), jnp.float32),
            pltpu.VMEM((n_sub, 2, tq, LANES), jnp.float32),
            pltpu.VMEM((n_sub, 2, tq, hd), jnp.float32),
        ] + [pltpu.VMEM((tq, tk), jnp.float32)] * 4 + [pltpu.VMEM((tq, LANES), jnp.float32)] * 4,
        compiler_params=pltpu.CompilerParams(
            dimension_semantics=("arbitrary", "arbitrary", "arbitrary"),
            vmem_limit_bytes=_vmem_limit(blk)),
        name="diff_attn",
    )(lq1, lk1, lq2, lk2, subw, qkv3, qkv3, qkv3, *cast)
    return outs[0], outs[1:]


def _neigh_bias(rpb, rows):
    heads = rpb.shape[0]
    rpb = rpb.astype(jnp.float32) * LOG2E
    qc = np.arange(GRID_W)[:, None]
    kc = np.arange(GRID_W)[None, :]
    col_start = np.clip(qc - NA_KW // 2, 0, GRID_W - NA_KW)
    col_ok = (kc >= col_start) & (kc < col_start + NA_KW)
    dc = kc - qc + (NA_KW - 1)
    tcol = jnp.full((heads, 2 * NA_KH - 1, GRID_W, GRID_W), NEG_BIG, jnp.float32)
    for v in range(2 * NA_KW - 1):
        tcol = jnp.where((col_ok & (dc == v))[None, None], rpb[:, :, v][:, :, None, None], tcol)
    strip = jnp.transpose(tcol, (0, 2, 1, 3)).reshape(heads, GRID_W, (2 * NA_KH - 1) * GRID_W)
    masked = lambda n: jnp.full((heads, GRID_W, n * GRID_W), NEG_BIG, jnp.float32)
    nblk = rows // NA_QROWS
    tables = []
    for rb, key_row0 in ((0, 0), (1, 0), (nblk - 1, (nblk - 3) * NA_QROWS)):
        lines = []
        for qr in range(NA_QROWS):
            r = rb * NA_QROWS + qr
            row_start = min(max(r - NA_KH // 2, 0), rows - NA_KH)
            kr_lo = max(row_start - key_row0, 0)
            kr_hi = min(row_start + NA_KH - 1 - key_row0, NA_KROWS - 1)
            dr_lo = key_row0 + kr_lo - r + NA_KH - 1
            pieces = [strip[:, :, dr_lo * GRID_W:(dr_lo + kr_hi - kr_lo + 1) * GRID_W]]
            if kr_lo > 0:
                pieces.insert(0, masked(kr_lo))
            if kr_hi < NA_KROWS - 1:
                pieces.append(masked(NA_KROWS - 1 - kr_hi))
            lines.append(jnp.concatenate(pieces, axis=-1))
        tables.append(jnp.concatenate(lines, axis=1))
    return jnp.stack(tables)


def _neigh_attn_kernel(bias0_ref, bias1_ref, q_ref, k_ref, v_ref, o_ref, *, nq, nk):
    d = NA_HEAD_DIM
    heads = q_ref.shape[2] // d
    step, last = pl.program_id(1), pl.num_programs(1) - 1
    offs = (jnp.where(step == last, nq, 0), jnp.where(step == 0, 0, nq))
    biases = (bias0_ref, bias1_ref)

    def scores(item, token):
        g, h = item
        cols = slice(h * d, (h + 1) * d)
        q = q_ref[0, g * nq:(g + 1) * nq, cols]
        k = k_ref[0, pl.ds(pl.multiple_of(offs[g], nq), nk), cols]
        if token is not None:
            k = k + token
        s = lax.dot_general(q, k, _NT, preferred_element_type=jnp.float32) + biases[g][0, h]
        return s, jnp.max(s, axis=1, keepdims=True)

    def consume(item, s, m):
        g, h = item
        cols = slice(h * d, (h + 1) * d)
        p = jnp.exp2(s - m)
        l = jnp.sum(p, axis=1, keepdims=True)
        v = v_ref[0, pl.ds(pl.multiple_of(offs[g], nq), nk), cols]
        o = jnp.dot(p.astype(v.dtype), v, preferred_element_type=jnp.float32)
        o_ref[0, g * nq:(g + 1) * nq, cols] = (o / l).astype(o_ref.dtype)
        return _zero_after(o, k_ref.dtype)

    items = [(g, h) for g in range(2) for h in range(heads)]
    nxt = scores(items[0], None)
    token = None
    for n, item in enumerate(items):
        cur = nxt
        if n + 1 < len(items):
            nxt = scores(items[n + 1], token)
        token = consume(item, *cur)


def _neigh_attn(qkv3, bias):
    b, seq, _ = qkv3.shape
    width = NA_HEADS * NA_HEAD_DIM
    nq = NA_QROWS * GRID_W
    nk = NA_KROWS * GRID_W
    nblk = seq // nq
    assert nblk % 2 == 0 and nblk >= 4
    col0 = 3 * DA_HEADS * 2 * DA_HEAD_DIM
    first_key = lambda s: jnp.clip(2 * s - 1, 0, nblk - 4) * nq
    class0 = lambda s: jnp.where(s == 0, 0, 1)
    class1 = lambda s: jnp.where(s == nblk // 2 - 1, 2, 1)
    kv_spec = lambda base: pl.BlockSpec((pl.Element(1), pl.Element(nk + nq), pl.Element(width)),
                                        lambda bi, s: (bi, first_key(s), col0 + base * width))
    bias_spec = lambda cls: pl.BlockSpec((1, NA_HEADS, nq, nk), lambda bi, s: (cls(s), 0, 0, 0))
    blk = 2 * (2 * NA_HEADS * nq * nk * 4 + 4 * nq * width * 2 + 2 * (nk + nq) * width * 2) + 8 * nq * nk * 4
    return pl.pallas_call(
        functools.partial(_neigh_attn_kernel, nq=nq, nk=nk),
        out_shape=jax.ShapeDtypeStruct((b, seq, width), jnp.bfloat16),
        grid=(b, nblk // 2),
        in_specs=[
            bias_spec(class0), bias_spec(class1),
            pl.BlockSpec((1, 2 * nq, width), lambda bi, s: (bi, s, col0 // width)),
            kv_spec(1), kv_spec(2),
        ],
        out_specs=pl.BlockSpec((1, 2 * nq, width), lambda bi, s: (bi, s, 0)),
        compiler_params=pltpu.CompilerParams(
            dimension_semantics=("parallel", "arbitrary"),
            vmem_limit_bytes=_vmem_limit(blk)),
        name="neigh_attn",
    )(bias, bias, qkv3, qkv3, qkv3)


def _gate_mix_kernel(h_ref, oa_ref, on_ref, wga_ref, wgb_ref, wa_ref, wb_ref, o_ref):
    f32 = jnp.float32
    h = h_ref[...]
    ga = 1.0 / (1.0 + jnp.exp(-jnp.dot(h, wga_ref[...], preferred_element_type=f32)))
    gb = 1.0 / (1.0 + jnp.exp(-jnp.dot(h, wgb_ref[...], preferred_element_type=f32)))
    a = jnp.dot(oa_ref[...], wa_ref[...], preferred_element_type=f32)
    b = jnp.dot(on_ref[...], wb_ref[...], preferred_element_type=f32)
    o_ref[...] = (ga * a + gb * b).astype(o_ref.dtype)


def _out_proj_kernel(x_ref, mixed_ref, gpost_ref, gnext_ref, wo_ref, o_ref, hn_ref):
    y = jnp.dot(mixed_ref[...], wo_ref[...], preferred_element_type=jnp.float32)
    x1 = x_ref[...] + _rms(y, gpost_ref[...])
    o_ref[...] = x1
    hn_ref[...] = _rms(x1, gnext_ref[...]).astype(hn_ref.dtype)


def _merge(x2, h2, gpost, gnext, oa2, on2, w_in_bf16, wa, wb, wo, *, tm, tn):
    m, d = x2.shape
    ca = oa2.shape[1]
    cb = on2.shape[1]
    gate_col0 = (w_in_bf16.shape[1] - 2 * d) // tn
    nj = d // tn
    blk = 2 * (tm * d * 2 + tm * (ca + cb) * 2 + (2 * d + ca + cb) * tn * 2 + tm * tn * 2) + 6 * tm * tn * 4
    mixed = pl.pallas_call(
        _gate_mix_kernel,
        out_shape=jax.ShapeDtypeStruct((m, d), jnp.bfloat16),
        grid=(nj, m // tm),
        in_specs=[
            pl.BlockSpec((tm, d), lambda j, i: (i, 0)),
            pl.BlockSpec((tm, ca), lambda j, i: (i, 0)),
            pl.BlockSpec((tm, cb), lambda j, i: (i, 0)),
            pl.BlockSpec((d, tn), lambda j, i: (0, gate_col0 + j)),
            pl.BlockSpec((d, tn), lambda j, i: (0, gate_col0 + nj + j)),
            pl.BlockSpec((ca, tn), lambda j, i: (0, j)),
            pl.BlockSpec((cb, tn), lambda j, i: (0, j)),
        ],
        out_specs=pl.BlockSpec((tm, tn), lambda j, i: (i, j)),
        compiler_params=pltpu.CompilerParams(
            dimension_semantics=("arbitrary", "arbitrary"),
            vmem_limit_bytes=_vmem_limit(blk)),
        name="gate_mix",
    )(h2, oa2, on2, w_in_bf16, w_in_bf16, wa, wb)
    blk = 2 * (2 * tm * d * 4 + 2 * tm * d * 2) + d * d * 2 + 2 * tm * d * 4
    return pl.pallas_call(
        _out_proj_kernel,
        out_shape=(jax.ShapeDtypeStruct((m, d), jnp.float32), jax.ShapeDtypeStruct((m, d), jnp.bfloat16)),
        grid=(m // tm,),
        in_specs=[
            pl.BlockSpec((tm, d), lambda i: (i, 0)),
            pl.BlockSpec((tm, d), lambda i: (i, 0)),
            pl.BlockSpec((1, d), lambda i: (0, 0)),
            pl.BlockSpec((1, d), lambda i: (0, 0)),
            pl.BlockSpec((d, d), lambda i: (0, 0), pipeline_mode=pl.Buffered(1)),
        ],
        out_specs=(pl.BlockSpec((tm, d), lambda i: (i, 0)), pl.BlockSpec((tm, d), lambda i: (i, 0))),
        compiler_params=pltpu.CompilerParams(
            dimension_semantics=("arbitrary",),
            vmem_limit_bytes=_vmem_limit(blk)),
        name="out_proj",
    )(x2, mixed, gpost, gnext, wo)


def _mlp_kernel(x_ref, h_ref, gpost_ref, wu_ref, wd_ref, o_ref, acc_ref):
    j = pl.program_id(1)

    def accumulate(first):
        u = jnp.dot(h_ref[...], wu_ref[...], preferred_element_type=jnp.float32)
        u = jnp.square(jnp.maximum(u, 0.0)).astype(wd_ref.dtype)
        part = jnp.dot(u, wd_ref[...], preferred_element_type=jnp.float32)
        acc_ref[...] = part if first else acc_ref[...] + part

    pl.when(j == 0)(lambda: accumulate(True))
    pl.when(j > 0)(lambda: accumulate(False))

    @pl.when(j == pl.num_programs(1) - 1)
    def _():
        o_ref[...] = x_ref[...] + _rms(acc_ref[...], gpost_ref[...])


def _mlp(x2, h2, gpost, wu, wd, *, tm, tf):
    m, d = x2.shape
    f = wu.shape[1]
    blk = 2 * (2 * tm * d * 4 + tm * d * 2 + 2 * d * tf * 2) + tm * d * 4 + 2 * tm * tf * 4
    return pl.pallas_call(
        _mlp_kernel,
        out_shape=jax.ShapeDtypeStruct((m, d), jnp.float32),
        grid=(m // tm, f // tf),
        in_specs=[
            pl.BlockSpec((tm, d), lambda i, j: (i, 0)),
            pl.BlockSpec((tm, d), lambda i, j: (i, 0)),
            pl.BlockSpec((1, d), lambda i, j: (0, 0)),
            pl.BlockSpec((d, tf), lambda i, j: (0, j)),
            pl.BlockSpec((tf, d), lambda i, j: (j, 0)),
        ],
        out_specs=pl.BlockSpec((tm, d), lambda i, j: (i, 0)),
        scratch_shapes=[pltpu.VMEM((tm, d), jnp.float32)],
        compiler_params=pltpu.CompilerParams(
            dimension_semantics=("parallel", "arbitrary"),
            vmem_limit_bytes=_vmem_limit(blk)),
        name="mlp",
    )(x2, h2, gpost, wu, wd)


def _rope_tables(seq):
    inv = (1.0 / (np.float32(ROPE_THETA) ** (np.arange(0, DA_HEAD_DIM, 2, dtype=np.float32) / DA_HEAD_DIM)))
    ang = np.arange(seq, dtype=np.float32)[:, None] * inv.astype(np.float32)[None, :]
    cos, sin = np.cos(ang).astype(np.float32), np.sin(ang).astype(np.float32)
    return jnp.asarray(np.concatenate([cos, cos], axis=1)), jnp.asarray(np.concatenate([-sin, sin], axis=1))


def kernel(x, w_in, w_branch_a, w_branch_b, w_out, norm_mix_pre, norm_mix_post, norm_mlp_pre, norm_mlp_post,
           lam_q1, lam_k1, lam_q2, lam_k2, subln_w, na_rpb, w_up, w_down):
    b, seq, d = x.shape
    m = b * seq
    depth = w_in.shape[0]
    da_width = DA_HEADS * 2 * DA_HEAD_DIM
    na_width = NA_HEADS * NA_HEAD_DIM
    qkv_cols = 3 * da_width + 3 * na_width
    bf16 = jnp.bfloat16
    cosd, sind = _rope_tables(seq)
    row = lambda v: v.reshape(1, -1).astype(jnp.float32)
    tm = min(512, seq)
    tq = min(512, seq)
    n_sub = min(2, seq // tq)
    x2 = x.reshape(m, d)
    for l in range(depth):
        lambda_init = 0.8 - 0.6 * math.exp(-0.3 * l)
        w_in_l = w_in[l].astype(bf16)
        qkv, h_mix = _qkv_proj(x2, row(norm_mix_pre[l]), w_in_l, cosd, sind, seq,
                               tm=min(1024, seq), tn=2048, da_width=da_width, na_width=na_width)
        qkv3 = qkv.reshape(b, seq, qkv_cols)
        late = [w_branch_a[l], w_branch_b[l], w_out[l], w_up[l], w_down[l]]
        steps = b * DA_HEADS * (seq // (tq * n_sub))
        inside = [_cast_rows_per_step(w.shape[0], steps) is not None for w in late]
        oa, done = _diff_attn(qkv3, row(lam_q1[l]), row(lam_k1[l]), row(lam_q2[l]), row(lam_k2[l]), row(subln_w[l]),
                              [w for w, ok in zip(late, inside) if ok],
                              tq=tq, tk=min(1024, seq), n_sub=n_sub, lambda_init=lambda_init)
        done = list(done)
        wa, wb, wo, wu, wd = [done.pop(0) if ok else w.astype(bf16) for w, ok in zip(late, inside)]
        on = _neigh_attn(qkv3, _neigh_bias(na_rpb[l], seq // GRID_W))
        x2, h_mlp = _merge(x2, h_mix, row(norm_mix_post[l]), row(norm_mlp_pre[l]), oa.reshape(m, da_width),
                           on.reshape(m, na_width), w_in_l, wa, wb, wo, tm=tm, tn=512)
        x2 = _mlp(x2, h_mlp, row(norm_mlp_post[l]), wu, wd, tm=tm, tf=min(1024, w_up.shape[2]))
    return x2.reshape(b, seq, d)
```

```python
import functools
import math

import jax
import jax.numpy as jnp
import numpy as np
from jax import lax
from jax.experimental import pallas as pl
from jax.experimental.pallas import tpu as pltpu

EPS = 1e-6
ROPE_THETA = 10000.0
GRID_W = 64
DA_HEADS = 4
DA_HEAD_DIM = 128
NA_HEADS = 8
NA_HEAD_DIM = 128
NA_KH = 8
NA_KW = 16
LANES = 128
SUBLANES = 8
BF16_ROWS = 16
MXU_TILE = 256
NEG_BIG = -1e30
NA_QROWS = 4
NA_KROWS = 12
OUT_PROJ_SUBTILES = 4
V7X_VMEM_BYTES = 64 * 1024 * 1024
assert DA_HEAD_DIM == NA_HEAD_DIM
LOG2E = math.log2(math.e)
QK_MULT = DA_HEAD_DIM ** -0.5 * LOG2E

_NT = (((1,), (1,)), ((), ()))


def _vmem_limit(block_bytes):
    return int(min(block_bytes + 16 * 1024 * 1024, V7X_VMEM_BYTES - 6 * 1024 * 1024))


def _rms(x, g):
    return x * lax.rsqrt(jnp.mean(x * x, axis=-1, keepdims=True) + EPS) * g


def _zero_after(x, dtype):
    bits = pltpu.bitcast(x[0:SUBLANES, 0:LANES], jnp.uint32)
    return pltpu.bitcast((bits >> 16) >> 16, jnp.float32)[0:1, :].astype(dtype)


def _qkv_kernel(x_ref, g_ref, w_ref, cos_ref, sin_ref, o_ref, h_ref, *, group_cols):
    j = pl.program_id(1)

    @pl.when(j == 0)
    def _():
        h_ref[...] = _rms(x_ref[...], g_ref[...]).astype(h_ref.dtype)

    acc = jnp.dot(h_ref[...], w_ref[...], preferred_element_type=jnp.float32)

    groups_per_tile = acc.shape[1] // group_cols
    for gi in range(groups_per_tile):
        group = j * groups_per_tile + gi
        rot = jnp.where(group == 0, QK_MULT, jnp.where(group == 1, 1.0, 0.0)).astype(jnp.float32)
        flat = jnp.where(group == 3, QK_MULT, jnp.where(group <= 1, 0.0, 1.0)).astype(jnp.float32)
        cs = cos_ref[...] * rot + flat
        sn = sin_ref[...] * rot
        for g in range(gi * group_cols // LANES, (gi + 1) * group_cols // LANES):
            y = acc[:, g * LANES:(g + 1) * LANES]
            o_ref[:, g * LANES:(g + 1) * LANES] = (y * cs + pltpu.roll(y, LANES // 2, 1) * sn).astype(o_ref.dtype)


def _qkv_proj(x2, gain, w_bf16, cosd, sind, seq, *, tm, tn, da_width, na_width):
    m, d = x2.shape
    assert da_width == na_width and tn % da_width == 0
    n_cols = 3 * da_width + 3 * na_width
    pos_blocks = seq // tm
    blk = 2 * (tm * d * 4 + d * tn * 2 + tm * tn * 2 + 2 * tm * LANES * 4 + tm * d * 2) + tm * tn * 4
    return pl.pallas_call(
        functools.partial(_qkv_kernel, group_cols=da_width),
        out_shape=(jax.ShapeDtypeStruct((m, n_cols), jnp.bfloat16), jax.ShapeDtypeStruct((m, d), jnp.bfloat16)),
        grid=(m // tm, n_cols // tn),
        in_specs=[
            pl.BlockSpec((tm, d), lambda i, j: (i, 0)),
            pl.BlockSpec((1, d), lambda i, j: (0, 0)),
            pl.BlockSpec((d, tn), lambda i, j: (0, j)),
            pl.BlockSpec((tm, LANES), lambda i, j: (i % pos_blocks, 0)),
            pl.BlockSpec((tm, LANES), lambda i, j: (i % pos_blocks, 0)),
        ],
        out_specs=(pl.BlockSpec((tm, tn), lambda i, j: (i, j)), pl.BlockSpec((tm, d), lambda i, j: (i, 0))),
        compiler_params=pltpu.CompilerParams(
            dimension_semantics=("parallel", "arbitrary"),
            vmem_limit_bytes=_vmem_limit(blk)),
        name="qkv_proj",
    )(x2, gain, w_bf16, cosd, sind)


def _diff_attn_kernel(*refs, tq, tk, lambda_init, n_cast):
    lq1_ref, lk1_ref, lq2_ref, lk2_ref, subw_ref, q_ref, k_ref, v_ref = refs[:8]
    cast_in = refs[8:8 + n_cast]
    o_ref = refs[8 + n_cast]
    cast_out = refs[9 + n_cast:9 + 2 * n_cast]
    m_ref, l_ref, a_ref, sa1, sa2, sb1, sb2, ma1, ma2, mb1, mb2 = refs[9 + 2 * n_cast:]
    for w_ref, wo_ref in zip(cast_in, cast_out):
        wo_ref[...] = w_ref[...].astype(wo_ref.dtype)
    seq = k_ref.shape[1]
    n_chunks = seq // tk
    n_sub = q_ref.shape[1] // tq
    d = DA_HEAD_DIM
    lam = (jnp.exp(jnp.sum(lq1_ref[...] * lk1_ref[...])) - jnp.exp(jnp.sum(lq2_ref[...] * lk2_ref[...]))
           + lambda_init)

    def scores(item, dst, token):
        sub, c = item
        for half in range(2):
            q = q_ref[0, sub * tq:(sub + 1) * tq, half * d:(half + 1) * d]
            mx = None
            for t in range(tk // MXU_TILE):
                k = k_ref[0, pl.ds(c * tk + t * MXU_TILE, MXU_TILE), half * d:(half + 1) * d]
                if token is not None:
                    k = k + token
                s = lax.dot_general(q, k, _NT, preferred_element_type=jnp.float32)
                dst[half][:, t * MXU_TILE:(t + 1) * MXU_TILE] = s
                for u in range(MXU_TILE // LANES):
                    su = s[:, u * LANES:(u + 1) * LANES]
                    mx = su if mx is None else jnp.maximum(mx, su)
            dst[2 + half][...] = jnp.broadcast_to(jnp.max(mx, axis=1, keepdims=True), mx.shape)

    def consume(item, src):
        sub, c = item
        token = None
        for half in range(2):
            if c == 0:
                m_prev = jnp.full((tq, LANES), NEG_BIG, jnp.float32)
            else:
                m_prev = m_ref[sub, half]
            m_new = jnp.maximum(m_prev, src[2 + half][...])
            alpha = jnp.exp2(m_prev - m_new)
            m_rep = jnp.concatenate([m_new] * (MXU_TILE // LANES), axis=1)
            psum = None
            pv = None
            for t in range(tk // MXU_TILE):
                p = jnp.exp2(src[half][:, t * MXU_TILE:(t + 1) * MXU_TILE] - m_rep)
                for u in range(MXU_TILE // LANES):
                    pu = p[:, u * LANES:(u + 1) * LANES]
                    psum = pu if psum is None else psum + pu
                v = v_ref[0, pl.ds(c * tk + t * MXU_TILE, MXU_TILE), :]
                pvt = jnp.dot(p.astype(v.dtype), v, preferred_element_type=jnp.float32)
                pv = pvt if pv is None else pv + pvt
            if c == 0:
                l_ref[sub, half] = psum
                a_ref[sub, half] = pv
            else:
                l_ref[sub, half] = alpha * l_ref[sub, half] + psum
                a_ref[sub, half] = a_ref[sub, half] * jnp.concatenate([alpha] * (2 * d // LANES), axis=1) + pv
            m_ref[sub, half] = m_new
            if half == 0:
                token = _zero_after(pv, k_ref.dtype)
        return token

    def finalize(sub):
        o1 = a_ref[sub, 0] / jnp.sum(l_ref[sub, 0], axis=1, keepdims=True)
        o2 = a_ref[sub, 1] / jnp.sum(l_ref[sub, 1], axis=1, keepdims=True)
        o = _rms(o1 - lam * o2, subw_ref[...]) * (1.0 - lambda_init)
        o_ref[0, sub * tq:(sub + 1) * tq, :] = o.astype(o_ref.dtype)

    bufs = ((sa1, sa2, ma1, ma2), (sb1, sb2, mb1, mb2))
    items = [(sub, c) for sub in range(n_sub) for c in range(n_chunks)]
    scores(items[0], bufs[0], None)
    token = None
    for n, item in enumerate(items):
        if n + 1 < len(items):
            scores(items[n + 1], bufs[(n + 1) % 2], token)
        token = consume(item, bufs[n % 2])
        if item[1] == n_chunks - 1:
            finalize(item[0])


def _cast_rows_per_step(rows, steps):
    r = max(BF16_ROWS, rows // steps)
    return r if rows % r == 0 and rows // r <= steps else None


def _diff_attn(qkv3, lq1, lk1, lq2, lk2, subw, cast, *, tq, tk, n_sub, lambda_init):
    b, seq, _ = qkv3.shape
    hd = 2 * DA_HEAD_DIM
    tb = n_sub * tq
    nq = seq // tb
    steps = b * DA_HEADS * nq
    kern = functools.partial(_diff_attn_kernel, tq=tq, tk=tk, lambda_init=lambda_init, n_cast=len(cast))
    vec = lambda n: pl.BlockSpec((1, n), lambda bi, h, qi: (0, 0))

    def cast_spec(w):
        r = _cast_rows_per_step(w.shape[0], steps)
        nb = w.shape[0] // r
        return pl.BlockSpec((r, w.shape[1]), lambda bi, h, qi: (((bi * DA_HEADS + h) * nq + qi) * nb // steps, 0))

    cast_specs = [cast_spec(w) for w in cast]
    blk = (2 * (2 * seq * hd * 2 + 2 * tb * hd * 2) + 2 * n_sub * (2 * tq * LANES * 4 + tq * hd * 4)
           + 6 * tq * tk * 4 + sum(2 * sp.block_shape[0] * sp.block_shape[1] * 6 for sp in cast_specs))
    outs = pl.pallas_call(
        kern,
        out_shape=[jax.ShapeDtypeStruct((b, seq, DA_HEADS * hd), jnp.bfloat16)]
        + [jax.ShapeDtypeStruct(w.shape, jnp.bfloat16) for w in cast],
        grid=(b, DA_HEADS, nq),
        in_specs=[
            vec(DA_HEAD_DIM), vec(DA_HEAD_DIM), vec(DA_HEAD_DIM), vec(DA_HEAD_DIM), vec(hd),
            pl.BlockSpec((1, tb, hd), lambda bi, h, qi: (bi, qi, h)),
            pl.BlockSpec((1, seq, hd), lambda bi, h, qi: (bi, 0, DA_HEADS + h)),
            pl.BlockSpec((1, seq, hd), lambda bi, h, qi: (bi, 0, 2 * DA_HEADS + h)),
        ] + cast_specs,
        out_specs=[pl.BlockSpec((1, tb, hd), lambda bi, h, qi: (bi, qi, h))] + cast_specs,
        scratch_shapes=[
            pltpu.VMEM((n_sub, 2, tq, LANES), jnp.float32),
            pltpu.VMEM((n_sub, 2, tq, LANES), jnp.float32),
            pltpu.VMEM((n_sub, 2, tq, hd), jnp.float32),
        ] + [pltpu.VMEM((tq, tk), jnp.float32)] * 4 + [pltpu.VMEM((tq, LANES), jnp.float32)] * 4,
        compiler_params=pltpu.CompilerParams(
            dimension_semantics=("arbitrary", "arbitrary", "arbitrary"),
            vmem_limit_bytes=_vmem_limit(blk)),
        name="diff_attn",
    )(lq1, lk1, lq2, lk2, subw, qkv3, qkv3, qkv3, *cast)
    return outs[0], outs[1:]


def _neigh_bias(rpb, rows):
    heads = rpb.shape[0]
    rpb = rpb.astype(jnp.float32) * LOG2E
    qc = np.arange(GRID_W)[:, None]
    kc = np.arange(GRID_W)[None, :]
    col_start = np.clip(qc - NA_KW // 2, 0, GRID_W - NA_KW)
    col_ok = (kc >= col_start) & (kc < col_start + NA_KW)
    dc = kc - qc + (NA_KW - 1)
    tcol = jnp.full((heads, 2 * NA_KH - 1, GRID_W, GRID_W), NEG_BIG, jnp.float32)
    for v in range(2 * NA_KW - 1):
        tcol = jnp.where((col_ok & (dc == v))[None, None], rpb[:, :, v][:, :, None, None], tcol)
    strip = jnp.transpose(tcol, (0, 2, 1, 3)).reshape(heads, GRID_W, (2 * NA_KH - 1) * GRID_W)
    masked = lambda n: jnp.full((heads, GRID_W, n * GRID_W), NEG_BIG, jnp.float32)
    nblk = rows // NA_QROWS
    tables = []
    for rb, key_row0 in ((0, 0), (1, 0), (nblk - 1, (nblk - 3) * NA_QROWS)):
        lines = []
        for qr in range(NA_QROWS):
            r = rb * NA_QROWS + qr
            row_start = min(max(r - NA_KH // 2, 0), rows - NA_KH)
            kr_lo = max(row_start - key_row0, 0)
            kr_hi = min(row_start + NA_KH - 1 - key_row0, NA_KROWS - 1)
            dr_lo = key_row0 + kr_lo - r + NA_KH - 1
            pieces = [strip[:, :, dr_lo * GRID_W:(dr_lo + kr_hi - kr_lo + 1) * GRID_W]]
            if kr_lo > 0:
                pieces.insert(0, masked(kr_lo))
            if kr_hi < NA_KROWS - 1:
                pieces.append(masked(NA_KROWS - 1 - kr_hi))
            lines.append(jnp.concatenate(pieces, axis=-1))
        tables.append(jnp.concatenate(lines, axis=1))
    return jnp.stack(tables)


def _neigh_attn_kernel(bias0_ref, bias1_ref, q_ref, k_ref, v_ref, o_ref, *, nq, nk):
    d = NA_HEAD_DIM
    heads = q_ref.shape[2] // d
    step, last = pl.program_id(1), pl.num_programs(1) - 1
    offs = (jnp.where(step == last, nq, 0), jnp.where(step == 0, 0, nq))
    biases = (bias0_ref, bias1_ref)

    def scores(item, token):
        g, h = item
        cols = slice(h * d, (h + 1) * d)
        q = q_ref[0, g * nq:(g + 1) * nq, cols]
        k = k_ref[0, pl.ds(pl.multiple_of(offs[g], nq), nk), cols]
        if token is not None:
            k = k + token
        s = lax.dot_general(q, k, _NT, preferred_element_type=jnp.float32) + biases[g][0, h]
        return s, jnp.max(s, axis=1, keepdims=True)

    def consume(item, s, m):
        g, h = item
        cols = slice(h * d, (h + 1) * d)
        p = jnp.exp2(s - m)
        l = jnp.sum(p, axis=1, keepdims=True)
        v = v_ref[0, pl.ds(pl.multiple_of(offs[g], nq), nk), cols]
        o = jnp.dot(p.astype(v.dtype), v, preferred_element_type=jnp.float32)
        o_ref[0, g * nq:(g + 1) * nq, cols] = (o / l).astype(o_ref.dtype)
        return _zero_after(o, k_ref.dtype)

    items = [(g, h) for g in range(2) for h in range(heads)]
    nxt = scores(items[0], None)
    token = None
    for n, item in enumerate(items):
        cur = nxt
        if n + 1 < len(items):
            nxt = scores(items[n + 1], token)
        token = consume(item, *cur)


def _neigh_attn(qkv3, bias):
    b, seq, _ = qkv3.shape
    width = NA_HEADS * NA_HEAD_DIM
    nq = NA_QROWS * GRID_W
    nk = NA_KROWS * GRID_W
    nblk = seq // nq
    assert nblk % 2 == 0 and nblk >= 4
    col0 = 3 * DA_HEADS * 2 * DA_HEAD_DIM
    first_key = lambda s: jnp.clip(2 * s - 1, 0, nblk - 4) * nq
    class0 = lambda s: jnp.where(s == 0, 0, 1)
    class1 = lambda s: jnp.where(s == nblk // 2 - 1, 2, 1)
    kv_spec = lambda base: pl.BlockSpec((pl.Element(1), pl.Element(nk + nq), pl.Element(width)),
                                        lambda bi, s: (bi, first_key(s), col0 + base * width))
    bias_spec = lambda cls: pl.BlockSpec((1, NA_HEADS, nq, nk), lambda bi, s: (cls(s), 0, 0, 0))
    blk = 2 * (2 * NA_HEADS * nq * nk * 4 + 4 * nq * width * 2 + 2 * (nk + nq) * width * 2) + 8 * nq * nk * 4
    return pl.pallas_call(
        functools.partial(_neigh_attn_kernel, nq=nq, nk=nk),
        out_shape=jax.ShapeDtypeStruct((b, seq, width), jnp.bfloat16),
        grid=(b, nblk // 2),
        in_specs=[
            bias_spec(class0), bias_spec(class1),
            pl.BlockSpec((1, 2 * nq, width), lambda bi, s: (bi, s, col0 // width)),
            kv_spec(1), kv_spec(2),
        ],
        out_specs=pl.BlockSpec((1, 2 * nq, width), lambda bi, s: (bi, s, 0)),
        compiler_params=pltpu.CompilerParams(
            dimension_semantics=("parallel", "arbitrary"),
            vmem_limit_bytes=_vmem_limit(blk)),
        name="neigh_attn",
    )(bias, bias, qkv3, qkv3, qkv3)


def _gate_mix_kernel(h_ref, oa_ref, on_ref, wga_ref, wgb_ref, wa_ref, wb_ref, o_ref):
    f32 = jnp.float32
    h = h_ref[...]
    ga = 1.0 / (1.0 + jnp.exp(-jnp.dot(h, wga_ref[...], preferred_element_type=f32)))
    gb = 1.0 / (1.0 + jnp.exp(-jnp.dot(h, wgb_ref[...], preferred_element_type=f32)))
    a = jnp.dot(oa_ref[...], wa_ref[...], preferred_element_type=f32)
    b = jnp.dot(on_ref[...], wb_ref[...], preferred_element_type=f32)
    o_ref[...] = (ga * a + gb * b).astype(o_ref.dtype)


def _out_proj_kernel(x_ref, mixed_ref, gpost_ref, gnext_ref, wo_ref, o_ref, hn_ref):
    rows = x_ref.shape[0] // OUT_PROJ_SUBTILES
    for t in range(OUT_PROJ_SUBTILES):
        sl = slice(t * rows, (t + 1) * rows)
        y = jnp.dot(mixed_ref[sl, :], wo_ref[...], preferred_element_type=jnp.float32)
        x1 = x_ref[sl, :] + _rms(y, gpost_ref[...])
        o_ref[sl, :] = x1
        hn_ref[sl, :] = _rms(x1, gnext_ref[...]).astype(hn_ref.dtype)


def _merge(x2, h2, gpost, gnext, oa2, on2, w_in_bf16, wa, wb, wo, *, tm, tn):
    m, d = x2.shape
    ca = oa2.shape[1]
    cb = on2.shape[1]
    gate_col0 = (w_in_bf16.shape[1] - 2 * d) // tn
    nj = d // tn
    blk = 2 * (tm * d * 2 + tm * (ca + cb) * 2 + (2 * d + ca + cb) * tn * 2 + tm * tn * 2) + 6 * tm * tn * 4
    mixed = pl.pallas_call(
        _gate_mix_kernel,
        out_shape=jax.ShapeDtypeStruct((m, d), jnp.bfloat16),
        grid=(nj, m // tm),
        in_specs=[
            pl.BlockSpec((tm, d), lambda j, i: (i, 0)),
            pl.BlockSpec((tm, ca), lambda j, i: (i, 0)),
            pl.BlockSpec((tm, cb), lambda j, i: (i, 0)),
            pl.BlockSpec((d, tn), lambda j, i: (0, gate_col0 + j)),
            pl.BlockSpec((d, tn), lambda j, i: (0, gate_col0 + nj + j)),
            pl.BlockSpec((ca, tn), lambda j, i: (0, j)),
            pl.BlockSpec((cb, tn), lambda j, i: (0, j)),
        ],
        out_specs=pl.BlockSpec((tm, tn), lambda j, i: (i, j)),
        compiler_params=pltpu.CompilerParams(
            dimension_semantics=("arbitrary", "arbitrary"),
            vmem_limit_bytes=_vmem_limit(blk)),
        name="gate_mix",
    )(h2, oa2, on2, w_in_bf16, w_in_bf16, wa, wb)
    blk = 2 * (2 * tm * d * 4 + 2 * tm * d * 2) + d * d * 2 + 2 * tm * d * 4
    return pl.pallas_call(
        _out_proj_kernel,
        out_shape=(jax.ShapeDtypeStruct((m, d), jnp.float32), jax.ShapeDtypeStruct((m, d), jnp.bfloat16)),
        grid=(m // tm,),
        in_specs=[
            pl.BlockSpec((tm, d), lambda i: (i, 0)),
            pl.BlockSpec((tm, d), lambda i: (i, 0)),
            pl.BlockSpec((1, d), lambda i: (0, 0)),
            pl.BlockSpec((1, d), lambda i: (0, 0)),
            pl.BlockSpec((d, d), lambda i: (0, 0), pipeline_mode=pl.Buffered(1)),
        ],
        out_specs=(pl.BlockSpec((tm, d), lambda i: (i, 0)), pl.BlockSpec((tm, d), lambda i: (i, 0))),
        compiler_params=pltpu.CompilerParams(
            dimension_semantics=("arbitrary",),
            vmem_limit_bytes=_vmem_limit(blk)),
        name="out_proj",
    )(x2, mixed, gpost, gnext, wo)


def _mlp_kernel(x_ref, h_ref, gpost_ref, wu_ref, wd_ref, o_ref, acc_ref):
    j = pl.program_id(1)

    def accumulate(first):
        u = jnp.dot(h_ref[...], wu_ref[...], preferred_element_type=jnp.float32)
        u = jnp.square(jnp.maximum(u, 0.0)).astype(wd_ref.dtype)
        part = jnp.dot(u, wd_ref[...], preferred_element_type=jnp.float32)
        acc_ref[...] = part if first else acc_ref[...] + part

    pl.when(j == 0)(lambda: accumulate(True))
    pl.when(j > 0)(lambda: accumulate(False))

    @pl.when(j == pl.num_programs(1) - 1)
    def _():
        o_ref[...] = x_ref[...] + _rms(acc_ref[...], gpost_ref[...])


def _mlp(x2, h2, gpost, wu, wd, *, tm, tf):
    m, d = x2.shape
    f = wu.shape[1]
    blk = 2 * (2 * tm * d * 4 + tm * d * 2 + 2 * d * tf * 2) + tm * d * 4 + 2 * tm * tf * 4
    return pl.pallas_call(
        _mlp_kernel,
        out_shape=jax.ShapeDtypeStruct((m, d), jnp.float32),
        grid=(m // tm, f // tf),
        in_specs=[
            pl.BlockSpec((tm, d), lambda i, j: (i, 0)),
            pl.BlockSpec((tm, d), lambda i, j: (i, 0)),
            pl.BlockSpec((1, d), lambda i, j: (0, 0)),
            pl.BlockSpec((d, tf), lambda i, j: (0, j)),
            pl.BlockSpec((tf, d), lambda i, j: (j, 0)),
        ],
        out_specs=pl.BlockSpec((tm, d), lambda i, j: (i, 0)),
        scratch_shapes=[pltpu.VMEM((tm, d), jnp.float32)],
        compiler_params=pltpu.CompilerParams(
            dimension_semantics=("parallel", "arbitrary"),
            vmem_limit_bytes=_vmem_limit(blk)),
        name="mlp",
    )(x2, h2, gpost, wu, wd)


def _rope_tables(seq):
    inv = (1.0 / (np.float32(ROPE_THETA) ** (np.arange(0, DA_HEAD_DIM, 2, dtype=np.float32) / DA_HEAD_DIM)))
    ang = np.arange(seq, dtype=np.float32)[:, None] * inv.astype(np.float32)[None, :]
    cos, sin = np.cos(ang).astype(np.float32), np.sin(ang).astype(np.float32)
    return jnp.asarray(np.concatenate([cos, cos], axis=1)), jnp.asarray(np.concatenate([-sin, sin], axis=1))


def kernel(x, w_in, w_branch_a, w_branch_b, w_out, norm_mix_pre, norm_mix_post, norm_mlp_pre, norm_mlp_post,
           lam_q1, lam_k1, lam_q2, lam_k2, subln_w, na_rpb, w_up, w_down):
    b, seq, d = x.shape
    m = b * seq
    depth = w_in.shape[0]
    da_width = DA_HEADS * 2 * DA_HEAD_DIM
    na_width = NA_HEADS * NA_HEAD_DIM
    qkv_cols = 3 * da_width + 3 * na_width
    bf16 = jnp.bfloat16
    cosd, sind = _rope_tables(seq)
    row = lambda v: v.reshape(1, -1).astype(jnp.float32)
    tm = min(512, seq)
    tq = min(512, seq)
    n_sub = min(2, seq // tq)
    x2 = x.reshape(m, d)
    for l in range(depth):
        lambda_init = 0.8 - 0.6 * math.exp(-0.3 * l)
        w_in_l = w_in[l].astype(bf16)
        qkv, h_mix = _qkv_proj(x2, row(norm_mix_pre[l]), w_in_l, cosd, sind, seq,
                               tm=min(1024, seq), tn=2048, da_width=da_width, na_width=na_width)
        qkv3 = qkv.reshape(b, seq, qkv_cols)
        late = [w_branch_a[l], w_branch_b[l], w_out[l], w_up[l], w_down[l]]
        steps = b * DA_HEADS * (seq // (tq * n_sub))
        inside = [_cast_rows_per_step(w.shape[0], steps) is not None for w in late]
        oa, done = _diff_attn(qkv3, row(lam_q1[l]), row(lam_k1[l]), row(lam_q2[l]), row(lam_k2[l]), row(subln_w[l]),
                              [w for w, ok in zip(late, inside) if ok],
                              tq=tq, tk=min(1024, seq), n_sub=n_sub, lambda_init=lambda_init)
        done = list(done)
        wa, wb, wo, wu, wd = [done.pop(0) if ok else w.astype(bf16) for w, ok in zip(late, inside)]
        on = _neigh_attn(qkv3, _neigh_bias(na_rpb[l], seq // GRID_W))
        x2, h_mlp = _merge(x2, h_mix, row(norm_mix_post[l]), row(norm_mlp_pre[l]), oa.reshape(m, da_width),
                           on.reshape(m, na_width), w_in_l, wa, wb, wo, tm=tm, tn=512)
        x2 = _mlp(x2, h_mlp, row(norm_mlp_post[l]), wu, wd, tm=tm, tf=min(1024, w_up.shape[2]))
    return x2.reshape(b, seq, d)
```
